```python
import math
import jax, jax.numpy as jnp
from jax import lax
import numpy as np

D_MODEL = 2048
BATCH = 2
SEQ = 4096
DEPTH = 4

CHUNK = 64
EPS = 1e-6
SSD_HEADDIM = 64
SSD_INNER = D_MODEL
SSD_HEADS = SSD_INNER // SSD_HEADDIM
SSD_GROUPS = 8
SSD_STATE = 128
SSD_CONV = 4
SSD_CONV_DIM = SSD_INNER + 2 * SSD_GROUPS * SSD_STATE
SC_WIDTH = D_MODEL // 2
SC_CONV = 3
ATT_HEADS = 8
ATT_HEADDIM = 128
ATT_WIDTH = ATT_HEADS * ATT_HEADDIM
ATT_SCALE = ATT_HEADDIM ** -0.5
IDX_HEADS = 16
IDX_DIM = 64
TOPK_MAX = 256
Q_BLOCK = 128
N_BRANCH = 3
FFN_HIDDEN = -(-8 * D_MODEL // (3 * 256)) * 256
IN_SIZES = (
    SSD_INNER,
    SSD_CONV_DIM,
    SSD_HEADS,
    SC_WIDTH, SC_WIDTH, SC_WIDTH,
    ATT_WIDTH, ATT_WIDTH, ATT_WIDTH,
    IDX_HEADS * IDX_DIM,
    IDX_DIM,
    IDX_HEADS,
    N_BRANCH * D_MODEL,
)
N_IN = sum(IN_SIZES)

kernel_name = "hybrid_ssd_shortconv_dsa_trunk"


def _split_points(sizes):
    pts, acc = [], 0
    for s in sizes[:-1]:
        acc += s
        pts.append(acc)
    return pts


def rmsnorm(x, g):
    xf = x.astype(jnp.float32)
    y = xf * lax.rsqrt(jnp.mean(xf * xf, axis=-1, keepdims=True) + EPS)
    return (y * g.astype(jnp.float32)).astype(x.dtype)


def causal_depthwise_conv(x, w):
    K, C = w.shape
    return lax.conv_general_dilated(
        x, w[:, None, :].astype(x.dtype), window_strides=(1,), padding=[(K - 1, 0)],
        dimension_numbers=("NWC", "WIO", "NWC"), feature_group_count=C)


def ssd_scan(x, dt, A, Bm, Cm):
    Bsz, L, H, P = x.shape
    G, N = Bm.shape[2], Bm.shape[3]
    R = H // G
    nc = L // CHUNK
    f32 = jnp.float32
    x = x.reshape(Bsz, nc, CHUNK, G, R, P).astype(f32)
    dt = dt.reshape(Bsz, nc, CHUNK, G, R)
    Bm = Bm.reshape(Bsz, nc, CHUNK, G, N).astype(f32)
    Cm = Cm.reshape(Bsz, nc, CHUNK, G, N).astype(f32)
    a_cs = jnp.cumsum(dt * A.reshape(G, R), axis=2)
    xdt = x * dt[..., None]
    seg = a_cs[:, :, :, None] - a_cs[:, :, None, :]
    causal = jnp.tril(jnp.ones((CHUNK, CHUNK), dtype=bool))
    decay = jnp.exp(jnp.where(causal[:, :, None, None], seg, -jnp.inf))
    cb = jnp.einsum("bctgn,bcsgn->bctsg", Cm, Bm)
    y_diag = jnp.einsum("bctsgr,bcsgrp->bctgrp", cb[..., None] * decay, xdt)
    decay_to_end = jnp.exp(a_cs[:, :, -1:] - a_cs)
    states = jnp.einsum("bcsgn,bcsgrp->bcgrpn", Bm, xdt * decay_to_end[..., None])
    chunk_decay = jnp.exp(a_cs[:, :, -1])

    def step(h, inp):
        s_c, d_c = inp
        return h * d_c[..., None, None] + s_c, h

    h0 = jnp.zeros((Bsz, G, R, P, N), f32)
    _, h_prev = lax.scan(step, h0, (jnp.moveaxis(states, 1, 0), jnp.moveaxis(chunk_decay, 1, 0)))
    h_prev = jnp.moveaxis(h_prev, 0, 1)
    y_off = jnp.einsum("bctgn,bcgrpn->bctgrp", Cm, h_prev) * jnp.exp(a_cs)[..., None]
    return (y_diag + y_off).reshape(Bsz, L, H, P)


def ssd_mixer(z, xbc, dt_raw, conv_w, conv_b, dt_bias, a_log, d_skip, norm_g):
    Bsz, L, _ = z.shape
    xbc = jax.nn.silu(causal_depthwise_conv(xbc, conv_w) + conv_b)
    xs, Bm, Cm = jnp.split(xbc, [SSD_INNER, SSD_INNER + SSD_GROUPS * SSD_STATE], axis=-1)
    xs = xs.reshape(Bsz, L, SSD_HEADS, SSD_HEADDIM)
    Bm = Bm.reshape(Bsz, L, SSD_GROUPS, SSD_STATE)
    Cm = Cm.reshape(Bsz, L, SSD_GROUPS, SSD_STATE)
    dt = jax.nn.softplus(dt_raw.astype(jnp.float32) + dt_bias.astype(jnp.float32))
    A = -jnp.exp(a_log.astype(jnp.float32))
    y = ssd_scan(xs, dt, A, Bm, Cm)
    y = y + d_skip.astype(jnp.float32)[:, None] * xs.astype(jnp.float32)
    y = y.reshape(Bsz, L, SSD_INNER)
    y = rmsnorm(y * jax.nn.silu(z.astype(jnp.float32)), norm_g)
    return y.astype(z.dtype)


def short_conv_mixer(b, cg, h, conv_w):
    return b * causal_depthwise_conv(cg * h, conv_w)


def dsa_attention(q, k, v, iq, ik, iw):
    Bsz, S, H, Dh = q.shape
    topk = min(TOPK_MAX, S // 4)
    nblk = S // Q_BLOCK
    key_chunk = jnp.arange(S) // CHUNK
    ik32 = ik.astype(jnp.float32)

    def block(i):
        start = i * Q_BLOCK
        qb = lax.dynamic_slice_in_dim(q, start, Q_BLOCK, axis=1).astype(jnp.float32)
        iqb = lax.dynamic_slice_in_dim(iq, start, Q_BLOCK, axis=1).astype(jnp.float32)
        iwb = lax.dynamic_slice_in_dim(iw, start, Q_BLOCK, axis=1).astype(jnp.float32)
        logits = jnp.einsum("bthd,bsd->bths", iqb, ik32)
        scores = jnp.einsum("bth,bths->bts", iwb, jax.nn.relu(logits))
        q_chunk = (start + jnp.arange(Q_BLOCK)) // CHUNK
        admissible = key_chunk[None, :] <= q_chunk[:, None]
        scores = jnp.where(admissible[None], scores, -jnp.inf)
        top_scores, idx = lax.top_k(scores, topk)
        valid = jnp.isfinite(top_scores)
        kb = jax.vmap(lambda kk, ii: kk[ii])(k, idx).astype(jnp.float32)
        vb = jax.vmap(lambda vv, ii: vv[ii])(v, idx).astype(jnp.float32)
        att = jnp.einsum("bthd,btkhd->bthk", qb, kb) * ATT_SCALE
        att = jnp.where(valid[:, :, None, :], att, -jnp.inf)
        p = jax.nn.softmax(att, axis=-1)
        return jnp.einsum("bthk,btkhd->bthd", p, vb).astype(q.dtype)

    outs = lax.map(block, jnp.arange(nblk))
    return jnp.moveaxis(outs, 0, 1).reshape(Bsz, S, H * Dh)


def hybrid_mixer(h, w_in, b_gate, ssd_conv_w, ssd_conv_b, ssd_dt_bias, ssd_a_log, ssd_d,
                 ssd_norm_g, sc_conv_w, w_br_ssd, w_br_sc, w_br_att, w_out):
    Bsz, S, D = h.shape
    proj = h @ w_in
    (z, xbc, dt_raw, sc_b, sc_c, sc_h, q, k, v, iq, ik, iw, gate_pre) = jnp.split(
        proj, _split_points(IN_SIZES), axis=-1)
    y_ssd = ssd_mixer(z, xbc, dt_raw, ssd_conv_w, ssd_conv_b, ssd_dt_bias, ssd_a_log,
                      ssd_d, ssd_norm_g)
    y_sc = short_conv_mixer(sc_b, sc_c, sc_h, sc_conv_w)
    y_att = dsa_attention(
        q.reshape(Bsz, S, ATT_HEADS, ATT_HEADDIM),
        k.reshape(Bsz, S, ATT_HEADS, ATT_HEADDIM),
        v.reshape(Bsz, S, ATT_HEADS, ATT_HEADDIM),
        iq.reshape(Bsz, S, IDX_HEADS, IDX_DIM), ik, iw)
    gates = jax.nn.sigmoid(gate_pre.reshape(Bsz, S, N_BRANCH, D) + b_gate)
    merged = (gates[:, :, 0] * (y_ssd @ w_br_ssd)
              + gates[:, :, 1] * (y_sc @ w_br_sc)
              + gates[:, :, 2] * (y_att @ w_br_att))
    return merged @ w_out


def swiglu(h, w_gate, w_up, w_down):
    return (jax.nn.silu(h @ w_gate) * (h @ w_up)) @ w_down


def setup_inputs(seed: int = 0) -> dict:
    key = jax.random.key(seed)
    ks = jax.random.split(key, 24)
    f32 = jnp.float32
    L, D = DEPTH, D_MODEL

    def nrm(k, shape, fan_in, scale=1.0):
        return jax.random.normal(k, shape, f32) * (scale * fan_in ** -0.5)

    def small(k, shape, s):
        return s * jax.random.normal(k, shape, f32)

    dt0 = jnp.exp(jax.random.uniform(ks[10], (L, SSD_HEADS), f32,
                                     minval=math.log(1e-3), maxval=math.log(1e-1)))
    return {
        "x": jax.random.normal(ks[0], (BATCH, SEQ, D), f32),
        "c": jax.random.normal(ks[1], (BATCH, D), f32),
        "w_ada": nrm(ks[2], (L, D, 6 * D), D, 0.5),
        "b_ada": small(ks[3], (L, 6 * D), 0.02),
        "g_mix": 1.0 + small(ks[4], (L, D), 0.02),
        "w_in": nrm(ks[5], (L, D, N_IN), D),
        "b_gate": small(ks[6], (L, N_BRANCH, D), 0.1),
        "ssd_conv_w": nrm(ks[7], (L, SSD_CONV, SSD_CONV_DIM), SSD_CONV),
        "ssd_conv_b": small(ks[8], (L, SSD_CONV_DIM), 0.02),
        "ssd_dt_bias": dt0 + jnp.log(-jnp.expm1(-dt0)),
        "ssd_a_log": jnp.log(jax.random.uniform(ks[11], (L, SSD_HEADS), f32, minval=1.0, maxval=16.0)),
        "ssd_d": 1.0 + small(ks[12], (L, SSD_HEADS), 0.1),
        "ssd_norm_g": 1.0 + small(ks[13], (L, SSD_INNER), 0.02),
        "sc_conv_w": nrm(ks[14], (L, SC_CONV, SC_WIDTH), SC_CONV),
        "w_br_ssd": nrm(ks[15], (L, SSD_INNER, D), SSD_INNER),
        "w_br_sc": nrm(ks[16], (L, SC_WIDTH, D), SC_WIDTH),
        "w_br_att": nrm(ks[17], (L, ATT_WIDTH, D), ATT_WIDTH),
        "w_out": nrm(ks[18], (L, D, D), D),
        "g_ffn": 1.0 + small(ks[19], (L, D), 0.02),
        "w_ffn_gate": nrm(ks[20], (L, D, FFN_HIDDEN), D),
        "w_ffn_up": nrm(ks[21], (L, D, FFN_HIDDEN), D),
        "w_ffn_down": nrm(ks[22], (L, FFN_HIDDEN, D), FFN_HIDDEN),
        "g_final": 1.0 + small(ks[23], (D,), 0.02),
    }


def reference(x, c, w_ada, b_ada, g_mix, w_in, b_gate, ssd_conv_w, ssd_conv_b, ssd_dt_bias,
              ssd_a_log, ssd_d, ssd_norm_g, sc_conv_w, w_br_ssd, w_br_sc, w_br_att, w_out,
              g_ffn, w_ffn_gate, w_ffn_up, w_ffn_down, g_final):
    cs = jax.nn.silu(c)
    for l in range(DEPTH):
        mod = cs @ w_ada[l] + b_ada[l]
        sh_m, sc_m, gt_m, sh_f, sc_f, gt_f = jnp.split(mod[:, None, :], 6, axis=-1)
        h = rmsnorm(x, g_mix[l]) * (1.0 + sc_m) + sh_m
        x = x + gt_m * hybrid_mixer(h, w_in[l], b_gate[l], ssd_conv_w[l], ssd_conv_b[l],
                                    ssd_dt_bias[l], ssd_a_log[l], ssd_d[l], ssd_norm_g[l],
                                    sc_conv_w[l], w_br_ssd[l], w_br_sc[l], w_br_att[l], w_out[l])
        h = rmsnorm(x, g_ffn[l]) * (1.0 + sc_f) + sh_f
        x = x + gt_f * swiglu(h, w_ffn_gate[l], w_ffn_up[l], w_ffn_down[l])
    return rmsnorm(x, g_final)
```

```python
import functools
import math

import jax
import jax.numpy as jnp
from jax import lax
from jax.experimental import pallas as pl
from jax.experimental.pallas import tpu as pltpu

F32 = jnp.float32
BF16 = jnp.bfloat16
I32 = jnp.int32

EPS = 1e-6
CHUNK = 64
SSD_HEADDIM = 64
SSD_HEADS = 32
SSD_GROUPS = 8
SSD_STATE = 128
SSD_HEADS_PER_GROUP = SSD_HEADS // SSD_GROUPS
SSD_CONV = 4
SC_CONV = 3
ATT_HEADS = 8
ATT_HEADDIM = 128
IDX_HEADS = 16
IDX_DIM = 64
TOPK_MAX = 256
N_BRANCH = 3

LANES = 128
BF16_SUBLANES = 16
VMEM_LIMIT_BYTES = 56 * 1024 * 1024

D = 2048
COL_Z = 0
COL_GATE = 2048
COL_XBC = 8192
COL_SCB = 12288
COL_SCC = 13312
COL_SCH = 14336
COL_Q = 15360
COL_K = 16384
COL_V = 17408
COL_IQ = 18432
COL_SMALL = 19456
COL_IK_ODD = 19584
N_PROJ = 19712
SMALL_DT = 64
SMALL_IW = 96

NEG_BIG = -1e30
INT_MIN = -2 ** 31
NEG_INF_KEY = -2139095041


def _cparams(sem):
    return pltpu.CompilerParams(dimension_semantics=sem, vmem_limit_bytes=VMEM_LIMIT_BYTES)


def _sigmoid(x):
    return 1.0 / (1.0 + jnp.exp(-x))


def _silu(x):
    return x * _sigmoid(x)


def _ada_kernel(cb_ref, w_ref, b_ref, o_ref, *, nb, tn):
    w = w_ref[...]
    for b in range(nb):
        c = cb_ref[b]
        cs = _silu(c)
        outs = []
        for j in range(tn // LANES):
            prod = w[:, j * LANES:(j + 1) * LANES] * cs
            outs.append(jnp.sum(prod, axis=0, keepdims=True))
        o_ref[b:b + 1, :] = jnp.concatenate(outs, axis=1) + b_ref[...]


def ada_mod(c, w_ada, b_ada, tn=1024):
    nl, d, n6 = w_ada.shape
    nb = c.shape[0]
    cb = jnp.broadcast_to(c[:, :, None], (nb, d, LANES))
    return pl.pallas_call(
        functools.partial(_ada_kernel, nb=nb, tn=tn),
        grid=(nl, n6 // tn),
        in_specs=[
            pl.BlockSpec((nb, d, LANES), lambda l, j: (0, 0, 0)),
            pl.BlockSpec((None, d, tn), lambda l, j: (l, 0, j)),
            pl.BlockSpec((None, 1, tn), lambda l, j: (l, 0, j)),
        ],
        out_specs=pl.BlockSpec((None, nb, tn), lambda l, j: (l, 0, j)),
        out_shape=jax.ShapeDtypeStruct((nl, nb, n6), F32),
        compiler_params=_cparams(("arbitrary", "arbitrary")),
        name="ada_mod",
    )(cb, w_ada, b_ada.reshape(nl, 1, n6))


def _norm_mod_kernel(x_ref, g_ref, mod_ref, o_ref, *, shift_row, scale_row):
    x = x_ref[...]
    y = x * lax.rsqrt(jnp.mean(x * x, axis=-1, keepdims=True) + EPS) * g_ref[...]
    sc = mod_ref[scale_row:scale_row + 1, :]
    sh = mod_ref[shift_row:shift_row + 1, :]
    o_ref[...] = (y * (1.0 + sc) + sh).astype(o_ref.dtype)


def norm_mod(x2, g, mod, seq, shift_row, scale_row, ts=512):
    t, d = x2.shape
    nsb = seq // ts
    return pl.pallas_call(
        functools.partial(_norm_mod_kernel, shift_row=shift_row, scale_row=scale_row),
        grid=(t // ts,),
        in_specs=[
            pl.BlockSpec((ts, d), lambda i: (i, 0)),
            pl.BlockSpec((1, d), lambda i: (0, 0)),
            pl.BlockSpec((None, 6, d), lambda i: (i // nsb, 0, 0)),
        ],
        out_specs=pl.BlockSpec((ts, d), lambda i: (i, 0)),
        out_shape=jax.ShapeDtypeStruct((t, d), BF16),
        compiler_params=_cparams(("parallel",)),
        name="norm_mod",
    )(x2, g.reshape(1, d), mod)


def _final_norm_kernel(x_ref, g_ref, o_ref):
    x = x_ref[...]
    o_ref[...] = x * lax.rsqrt(jnp.mean(x * x, axis=-1, keepdims=True) + EPS) * g_ref[...]


def final_norm(x2, g, ts=512):
    t, d = x2.shape
    return pl.pallas_call(
        _final_norm_kernel,
        grid=(t // ts,),
        in_specs=[pl.BlockSpec((ts, d), lambda i: (i, 0)), pl.BlockSpec((1, d), lambda i: (0, 0))],
        out_specs=pl.BlockSpec((ts, d), lambda i: (i, 0)),
        out_shape=jax.ShapeDtypeStruct((t, d), F32),
        compiler_params=_cparams(("parallel",)),
        name="final_norm",
    )(x2, g.reshape(1, d))


def _mm_kernel(a_ref, w_ref, o_ref):
    o_ref[...] = jnp.dot(a_ref[...], w_ref[...], preferred_element_type=F32).astype(o_ref.dtype)


def matmul_bf16(a, w, tm, tn):
    m, k = a.shape
    n = w.shape[1]
    return pl.pallas_call(
        _mm_kernel,
        grid=(m // tm, n // tn),
        in_specs=[pl.BlockSpec((tm, k), lambda i, j: (i, 0)), pl.BlockSpec((k, tn), lambda i, j: (0, j))],
        out_specs=pl.BlockSpec((tm, tn), lambda i, j: (i, j)),
        out_shape=jax.ShapeDtypeStruct((m, n), BF16),
        compiler_params=_cparams(("parallel", "arbitrary")),
        name="in_proj",
    )(a, w)


HALO = BF16_SUBLANES


def _fill_window(scr, x_ref, halo_ref, first, ts):
    halo = halo_ref[...].astype(F32)
    scr[0:HALO, :] = jnp.where(first, 0.0, halo)
    scr[HALO:HALO + ts, :] = x_ref[...].astype(F32)


def _ssd_conv_kernel(x_ref, halo_ref, w_ref, b_ref, o_ref, scr, *, ts):
    _fill_window(scr, x_ref, halo_ref, pl.program_id(1) == 0, ts)
    acc = jnp.broadcast_to(b_ref[...], (ts, b_ref.shape[1]))
    for k in range(SSD_CONV):
        off = HALO - (SSD_CONV - 1) + k
        acc = acc + w_ref[k:k + 1, :] * scr[off:off + ts, :]
    o_ref[...] = _silu(acc).astype(o_ref.dtype)


def ssd_conv(proj, conv_w, conv_b, nb, seq, ts=512, tc=1024):
    t = proj.shape[0]
    cw = conv_w.shape[1]
    nsb = seq // ts
    col0 = COL_XBC // tc

    def halo_map(b, i, j):
        return (jnp.maximum((b * seq + i * ts) // HALO - 1, 0), col0 + j)

    return pl.pallas_call(
        functools.partial(_ssd_conv_kernel, ts=ts),
        grid=(nb, nsb, cw // tc),
        in_specs=[
            pl.BlockSpec((ts, tc), lambda b, i, j: (b * nsb + i, col0 + j)),
            pl.BlockSpec((HALO, tc), halo_map),
            pl.BlockSpec((SSD_CONV, tc), lambda b, i, j: (0, j)),
            pl.BlockSpec((1, tc), lambda b, i, j: (0, j)),
        ],
        out_specs=pl.BlockSpec((ts, tc), lambda b, i, j: (b * nsb + i, j)),
        out_shape=jax.ShapeDtypeStruct((t, cw), BF16),
        scratch_shapes=[pltpu.VMEM((HALO + ts, tc), F32)],
        compiler_params=_cparams(("parallel", "parallel", "parallel")),
        name="ssd_conv",
    )(proj, proj, conv_w, conv_b.reshape(1, cw))


def _short_conv_kernel(b_ref, c_ref, h_ref, chalo_ref, hhalo_ref, w_ref, o_ref, scr, *, ts):
    first = pl.program_id(1) == 0
    halo = chalo_ref[...].astype(F32) * hhalo_ref[...].astype(F32)
    scr[0:HALO, :] = jnp.where(first, 0.0, halo)
    scr[HALO:HALO + ts, :] = c_ref[...].astype(F32) * h_ref[...].astype(F32)
    acc = None
    for k in range(SC_CONV):
        off = HALO - (SC_CONV - 1) + k
        term = w_ref[k:k + 1, :] * scr[off:off + ts, :]
        acc = term if acc is None else acc + term
    o_ref[...] = (b_ref[...].astype(F32) * acc).astype(o_ref.dtype)


def short_conv(proj, conv_w, nb, seq, ts=512):
    t = proj.shape[0]
    w = conv_w.shape[1]
    nsb = seq // ts
    cb, cc, ch = COL_SCB // w, COL_SCC // w, COL_SCH // w

    def halo_row(b, i):
        return jnp.maximum((b * seq + i * ts) // HALO - 1, 0)

    return pl.pallas_call(
        functools.partial(_short_conv_kernel, ts=ts),
        grid=(nb, nsb),
        in_specs=[
            pl.BlockSpec((ts, w), lambda b, i: (b * nsb + i, cb)),
            pl.BlockSpec((ts, w), lambda b, i: (b * nsb + i, cc)),
            pl.BlockSpec((ts, w), lambda b, i: (b * nsb + i, ch)),
            pl.BlockSpec((HALO, w), lambda b, i: (halo_row(b, i), cc)),
            pl.BlockSpec((HALO, w), lambda b, i: (halo_row(b, i), ch)),
            pl.BlockSpec((SC_CONV, w), lambda b, i: (0, 0)),
        ],
        out_specs=pl.BlockSpec((ts, w), lambda b, i: (b * nsb + i, 0)),
        out_shape=jax.ShapeDtypeStruct((t, w), BF16),
        scratch_shapes=[pltpu.VMEM((HALO + ts, w), F32)],
        compiler_params=_cparams(("parallel", "parallel")),
        name="short_conv",
    )(proj, proj, proj, proj, proj, conv_w)


def _split3(a):
    a1 = a.astype(BF16)
    r1 = a - a1.astype(F32)
    a2 = r1.astype(BF16)
    r2 = r1 - a2.astype(F32)
    return a1, a2, r2.astype(BF16)


def _sel_dot_left(sel, a):
    out = None
    for p in _split3(a):
        term = jnp.dot(sel, p, preferred_element_type=F32)
        out = term if out is None else out + term
    return out


def _sel_dot_right(a, sel):
    out = None
    for p in _split3(a):
        term = jnp.dot(p, sel, preferred_element_type=F32)
        out = term if out is None else out + term
    return out


def _dot_nt(a, b):
    return lax.dot_general(a, b, (((1,), (1,)), ((), ())), preferred_element_type=F32)


def _dot_tn(a, b):
    return lax.dot_general(a, b, (((0,), (0,)), ((), ())), preferred_element_type=F32)


SSD_ROWS = 2 * CHUNK


def _ssd_kernel(xs_ref, b_ref, c_ref, small_ref, z_ref, dtb_ref, alog_ref, dexp_ref, ng_ref, e_ref,
                o_ref, h_scr, y_scr):
    q = CHUNK
    hp = SSD_HEADS_PER_GROUP * SSD_HEADDIM
    width = SSD_HEADS * SSD_HEADDIM

    @pl.when(pl.program_id(1) == 0)
    def _():
        h_scr[...] = jnp.zeros_like(h_scr)

    lane = lax.broadcasted_iota(I32, (1, LANES), 1)
    dt_lane = (lane >= SMALL_DT) & (lane < SMALL_DT + SSD_HEADS)
    a_neg = jnp.where(dt_lane, -jnp.exp(alog_ref[...]), 0.0)

    row = lax.broadcasted_iota(I32, (q, q), 0)
    col = lax.broadcasted_iota(I32, (q, q), 1)
    tril = jnp.where(col <= row, 1.0, 0.0).astype(BF16)
    ones = jnp.ones((q, q), BF16)
    rowt = lax.broadcasted_iota(I32, (q, width), 0)
    colt = lax.broadcasted_iota(I32, (q, width), 1) % q
    eye_t = colt == rowt
    causal_t = colt <= rowt
    rbd = lax.broadcasted_iota(I32, (hp, hp), 0) // q
    cbd = lax.broadcasted_iota(I32, (hp, hp), 1) // SSD_HEADDIM
    bd_mask = rbd == cbd
    e = e_ref[...]

    for ci in range(SSD_ROWS // q):
        r0 = ci * q
        sm = small_ref[r0:r0 + q, :].astype(F32)
        pre = sm + dtb_ref[...]
        dt = jnp.where(dt_lane, jnp.maximum(pre, 0.0) + jnp.log(1.0 + jnp.exp(-jnp.abs(pre))), 0.0)
        a_cs = _sel_dot_left(tril, dt * a_neg)
        a_col = _sel_dot_right(a_cs, e)
        dtx = _sel_dot_right(dt, e)
        a_row = _sel_dot_left(ones, jnp.where(eye_t, a_col, 0.0))
        decay = jnp.exp(jnp.where(causal_t, a_col - a_row, -jnp.inf))
        a_last = a_col[q - 1:q, :]
        to_end = jnp.exp(a_last - a_col)
        from_start = jnp.exp(a_col)
        chunk_decay = jnp.exp(a_last)

        xf = xs_ref[r0:r0 + q, :].astype(F32)
        xdt = xf * dtx
        xdt_b = xdt.astype(BF16)
        xend_b = (xdt * to_end).astype(BF16)

        for g in range(SSD_GROUPS):
            bg = b_ref[r0:r0 + q, g * SSD_STATE:(g + 1) * SSD_STATE]
            cg = c_ref[r0:r0 + q, g * SSD_STATE:(g + 1) * SSD_STATE]
            sl = slice(g * hp, (g + 1) * hp)
            b4 = jnp.concatenate([bg] * SSD_HEADS_PER_GROUP, axis=0)
            cb_t = _dot_nt(cg, b4)
            m_g = (cb_t * decay[:, sl]).astype(BF16)
            xg = xdt_b[:, sl]
            x4 = jnp.concatenate([xg] * SSD_HEADS_PER_GROUP, axis=0)
            x_bd = jnp.where(bd_mask, x4, jnp.zeros_like(x4))
            y_diag = jnp.dot(m_g, x_bd, preferred_element_type=F32)
            h_g = h_scr[g]
            y_off = jnp.dot(cg, h_g.astype(BF16), preferred_element_type=F32) * from_start[:, sl]
            h_scr[g] = h_g * chunk_decay[:, sl] + _dot_tn(bg, xend_b[:, sl])
            y_scr[r0:r0 + q, sl] = y_diag + y_off

        y = y_scr[r0:r0 + q, :] + dexp_ref[...] * xf
        gated = y * _silu(z_ref[r0:r0 + q, :].astype(F32))
        out = gated * lax.rsqrt(jnp.mean(gated * gated, axis=-1, keepdims=True) + EPS) * ng_ref[...]
        o_ref[r0:r0 + q, :] = out.astype(o_ref.dtype)


def ssd_scan(xbc, proj, dt_bias, a_log, d_skip, norm_g, nb, seq):
    t = proj.shape[0]
    width = SSD_HEADS * SSD_HEADDIM
    gw = SSD_GROUPS * SSD_STATE
    rows = SSD_ROWS
    nsb = seq // rows

    def pad_small(v):
        return jnp.zeros((1, LANES), F32).at[0, SMALL_DT:SMALL_DT + SSD_HEADS].set(v.astype(F32))

    h_idx = jnp.arange(LANES)[:, None] - SMALL_DT
    c_idx = jnp.arange(width)[None, :] // SSD_HEADDIM
    e = (h_idx == c_idx).astype(BF16)
    dexp = jnp.repeat(d_skip.astype(F32), SSD_HEADDIM).reshape(1, width)

    row_map = lambda b, i: (b * nsb + i, 0)
    const = lambda b, i: (0, 0)
    return pl.pallas_call(
        _ssd_kernel,
        grid=(nb, nsb),
        in_specs=[
            pl.BlockSpec((rows, width), row_map),
            pl.BlockSpec((rows, gw), lambda b, i: (b * nsb + i, width // gw)),
            pl.BlockSpec((rows, gw), lambda b, i: (b * nsb + i, width // gw + 1)),
            pl.BlockSpec((rows, LANES), lambda b, i: (b * nsb + i, COL_SMALL // LANES)),
            pl.BlockSpec((rows, width), lambda b, i: (b * nsb + i, COL_Z // width)),
            pl.BlockSpec((1, LANES), const),
            pl.BlockSpec((1, LANES), const),
            pl.BlockSpec((1, width), const),
            pl.BlockSpec((1, width), const),
            pl.BlockSpec((LANES, width), const),
        ],
        out_specs=pl.BlockSpec((rows, width), row_map),
        out_shape=jax.ShapeDtypeStruct((t, width), BF16),
        scratch_shapes=[
            pltpu.VMEM((SSD_GROUPS, SSD_STATE, SSD_HEADS_PER_GROUP * SSD_HEADDIM), F32),
            pltpu.VMEM((rows, width), F32),
        ],
        compiler_params=_cparams(("parallel", "arbitrary")),
        name="ssd_scan",
    )(xbc, xbc, xbc, proj, proj, pad_small(dt_bias), pad_small(a_log), dexp,
      norm_g.reshape(1, width).astype(F32), e)


def _float_key(s):
    bits = pltpu.bitcast(s, I32)
    return bits ^ ((bits >> 31) & 0x7FFFFFFF)


def _dsa_kernel(q_ref, k_ref, v_ref, iq_ref, smq_ref, smk_ref, iko_ref, o_ref,
                key_scr, wb_scr, m_scr, l_scr, acc_scr, j_scr, *, tq, tk, seq, topk):
    qi = pl.program_id(1)
    q0 = qi * tq
    kend = q0 + tq
    nt = (kend + tk - 1) // tk
    idx_bits = max(1, (seq - 1).bit_length())

    smq = smq_ref[...].astype(F32)
    for h in range(IDX_HEADS):
        wb_scr[h] = jnp.broadcast_to(smq[:, SMALL_IW + h:SMALL_IW + h + 1], (tq, LANES))
    lane = lax.broadcasted_iota(I32, (tk, LANES), 1)
    q_chunk = (q0 + lax.broadcasted_iota(I32, (tq, tk), 0)) // CHUNK
    k_local = lax.broadcasted_iota(I32, (tq, tk), 1)
    nrep = tk // LANES

    def score_tile(kt, carry):
        k0 = pl.multiple_of(kt * tk, tk)
        smk = smk_ref[pl.ds(k0, tk), :]
        ik_even = jnp.where(lane < IDX_DIM, smk, jnp.zeros_like(smk))
        ik_odd = iko_ref[pl.ds(k0, tk), :]
        acc = jnp.zeros((tq, tk), F32)
        for hp in range(IDX_HEADS // 2):
            iq2 = iq_ref[:, hp * LANES:(hp + 1) * LANES]
            le = jnp.maximum(_dot_nt(iq2, ik_even), 0.0)
            lo = jnp.maximum(_dot_nt(iq2, ik_odd), 0.0)
            we = jnp.concatenate([wb_scr[2 * hp]] * nrep, axis=1)
            wo = jnp.concatenate([wb_scr[2 * hp + 1]] * nrep, axis=1)
            acc = acc + we * le + wo * lo
        admissible = (k0 + k_local) // CHUNK <= q_chunk
        key_scr[:, pl.ds(k0, tk)] = jnp.where(admissible, _float_key(acc), NEG_INF_KEY)
        return carry

    lax.fori_loop(0, nt, score_tile, 0)

    def count_rows(pred):
        def body(kt, cnt):
            k0 = pl.multiple_of(kt * tk, tk)
            hit = pred(key_scr[:, pl.ds(k0, tk)], k0)
            for j in range(nrep):
                cnt = cnt + jnp.where(hit[:, j * LANES:(j + 1) * LANES], 1, 0)
            return cnt
        cnt = lax.fori_loop(0, nt, body, jnp.zeros((tq, LANES), I32))
        return jnp.sum(cnt, axis=1, keepdims=True)

    def radix_step(it, t_u):
        c_u = t_u | lax.shift_left(jnp.int32(1), 31 - it)
        c_s = c_u ^ INT_MIN
        cnt = count_rows(lambda keys, k0: keys >= c_s)
        return jnp.where(cnt >= topk, c_u, t_u)

    t_u = lax.fori_loop(0, 32, radix_step, jnp.zeros((tq, 1), I32))
    thr = t_u ^ INT_MIN

    n_gt = count_rows(lambda keys, k0: keys > thr)
    n_eq = count_rows(lambda keys, k0: keys == thr)
    need = topk - n_gt
    excess = (n_eq > need) & (thr > NEG_INF_KEY)
    j_scr[...] = jnp.full((tq, 1), seq, I32)

    @pl.when(jnp.max(jnp.where(excess, 1, 0)) > 0)
    def _():
        def idx_step(it, jb):
            c = jb | lax.shift_left(jnp.int32(1), idx_bits - 1 - it)
            cnt = count_rows(lambda keys, k0: (keys == thr) & ((k0 + k_local) < c))
            return jnp.where(cnt < need, c, jb)
        jb = lax.fori_loop(0, idx_bits, idx_step, jnp.zeros((tq, 1), I32))
        j_scr[...] = jnp.where(excess, jb, seq)

    j_lim = j_scr[...]

    m_scr[...] = jnp.full(m_scr.shape, NEG_BIG, F32)
    l_scr[...] = jnp.zeros_like(l_scr)
    acc_scr[...] = jnp.zeros_like(acc_scr)
    scale = ATT_HEADDIM ** -0.5

    def attn_tile(kt, carry):
        k0 = pl.multiple_of(kt * tk, tk)
        keys = key_scr[:, pl.ds(k0, tk)]
        sel = (keys > thr) | ((keys == thr) & ((k0 + k_local) <= j_lim))
        sel = sel & (keys > NEG_INF_KEY)
        for h in range(ATT_HEADS):
            hs = slice(h * ATT_HEADDIM, (h + 1) * ATT_HEADDIM)
            qh = (q_ref[:, hs].astype(F32) * scale).astype(BF16)
            s = _dot_nt(qh, k_ref[pl.ds(k0, tk), hs])
            s = jnp.where(sel, s, NEG_BIG)
            m_old = m_scr[h]
            m_new = jnp.maximum(m_old, jnp.max(s, axis=1, keepdims=True))
            alpha = jnp.exp(m_old - m_new)
            p = jnp.exp(s - m_new)
            l_scr[h] = alpha * l_scr[h] + jnp.sum(p, axis=1, keepdims=True)
            pv = jnp.dot(p.astype(BF16), v_ref[pl.ds(k0, tk), hs], preferred_element_type=F32)
            acc_scr[:, hs] = alpha * acc_scr[:, hs] + pv
            m_scr[h] = m_new
        return carry

    lax.fori_loop(0, nt, attn_tile, 0)
    for h in range(ATT_HEADS):
        hs = slice(h * ATT_HEADDIM, (h + 1) * ATT_HEADDIM)
        o_ref[:, hs] = (acc_scr[:, hs] / l_scr[h]).astype(o_ref.dtype)


def dsa_attention(proj, nb, seq, tq=128, tk=512):
    t = proj.shape[0]
    w = ATT_HEADS * ATT_HEADDIM
    tk = min(tk, seq)
    nqb = seq // tq
    topk = min(TOPK_MAX, seq // 4)
    qrow = lambda b, i: b * nqb + i
    return pl.pallas_call(
        functools.partial(_dsa_kernel, tq=tq, tk=tk, seq=seq, topk=topk),
        grid=(nb, nqb),
        in_specs=[
            pl.BlockSpec((tq, w), lambda b, i: (qrow(b, i), COL_Q // w)),
            pl.BlockSpec((seq, w), lambda b, i: (b, COL_K // w)),
            pl.BlockSpec((seq, w), lambda b, i: (b, COL_V // w)),
            pl.BlockSpec((tq, w), lambda b, i: (qrow(b, i), COL_IQ // w)),
            pl.BlockSpec((tq, LANES), lambda b, i: (qrow(b, i), COL_SMALL // LANES)),
            pl.BlockSpec((seq, LANES), lambda b, i: (b, COL_SMALL // LANES)),
            pl.BlockSpec((seq, LANES), lambda b, i: (b, COL_IK_ODD // LANES)),
        ],
        out_specs=pl.BlockSpec((tq, w), lambda b, i: (qrow(b, i), 0)),
        out_shape=jax.ShapeDtypeStruct((t, w), BF16),
        scratch_shapes=[
            pltpu.VMEM((tq, seq), I32),
            pltpu.VMEM((IDX_HEADS, tq, LANES), F32),
            pltpu.VMEM((ATT_HEADS, tq, 1), F32),
            pltpu.VMEM((ATT_HEADS, tq, 1), F32),
            pltpu.VMEM((tq, w), F32),
            pltpu.VMEM((tq, 1), I32),
        ],
        compiler_params=_cparams(("parallel", "arbitrary")),
        name="dsa",
    )(proj, proj, proj, proj, proj, proj, proj)


def _merge_kernel(ys_ref, yc_ref, ya_ref, g0_ref, g1_ref, g2_ref, bg_ref, w1_ref, w2_ref, w3_ref, o_ref):
    def gate(g_ref, r):
        return _sigmoid(g_ref[...].astype(F32) + bg_ref[r:r + 1, :])

    acc = gate(g0_ref, 0) * jnp.dot(ys_ref[...], w1_ref[...], preferred_element_type=F32)
    acc = acc + gate(g1_ref, 1) * jnp.dot(yc_ref[...], w2_ref[...], preferred_element_type=F32)
    acc = acc + gate(g2_ref, 2) * jnp.dot(ya_ref[...], w3_ref[...], preferred_element_type=F32)
    o_ref[...] = acc.astype(o_ref.dtype)


def merge_branches(y_ssd, y_sc, y_att, proj, b_gate, w1, w2, w3, tm=1024, tn=512):
    t, d = y_ssd.shape[0], w1.shape[1]
    gcol = COL_GATE // tn
    gstep = d // tn
    return pl.pallas_call(
        _merge_kernel,
        grid=(t // tm, d // tn),
        in_specs=[
            pl.BlockSpec((tm, y_ssd.shape[1]), lambda i, j: (i, 0)),
            pl.BlockSpec((tm, y_sc.shape[1]), lambda i, j: (i, 0)),
            pl.BlockSpec((tm, y_att.shape[1]), lambda i, j: (i, 0)),
            pl.BlockSpec((tm, tn), lambda i, j: (i, gcol + j)),
            pl.BlockSpec((tm, tn), lambda i, j: (i, gcol + gstep + j)),
            pl.BlockSpec((tm, tn), lambda i, j: (i, gcol + 2 * gstep + j)),
            pl.BlockSpec((N_BRANCH, tn), lambda i, j: (0, j)),
            pl.BlockSpec((w1.shape[0], tn), lambda i, j: (0, j)),
            pl.BlockSpec((w2.shape[0], tn), lambda i, j: (0, j)),
            pl.BlockSpec((w3.shape[0], tn), lambda i, j: (0, j)),
        ],
        out_specs=pl.BlockSpec((tm, tn), lambda i, j: (i, j)),
        out_shape=jax.ShapeDtypeStruct((t, d), BF16),
        compiler_params=_cparams(("parallel", "arbitrary")),
        name="merge",
    )(y_ssd, y_sc, y_att, proj, proj, proj, b_gate, w1, w2, w3)


def _proj_residual_kernel(a_ref, w_ref, x_ref, mod_ref, o_ref, *, gate_row):
    y = jnp.dot(a_ref[...], w_ref[...], preferred_element_type=F32)
    o_ref[...] = x_ref[...] + mod_ref[gate_row:gate_row + 1, :] * y


def proj_residual(a, w, x2, mod, seq, gate_row, tm, tn, name):
    t, k = a.shape
    d = w.shape[1]
    nsb = seq // tm
    return pl.pallas_call(
        functools.partial(_proj_residual_kernel, gate_row=gate_row),
        grid=(t // tm, d // tn),
        in_specs=[
            pl.BlockSpec((tm, k), lambda i, j: (i, 0)),
            pl.BlockSpec((k, tn), lambda i, j: (0, j)),
            pl.BlockSpec((tm, tn), lambda i, j: (i, j)),
            pl.BlockSpec((None, 6, tn), lambda i, j: (i // nsb, 0, j)),
        ],
        out_specs=pl.BlockSpec((tm, tn), lambda i, j: (i, j)),
        out_shape=jax.ShapeDtypeStruct((t, d), F32),
        compiler_params=_cparams(("parallel", "arbitrary")),
        name=name,
    )(a, w, x2, mod)


def _ffn_up_kernel(h_ref, wg_ref, wu_ref, o_ref):
    h = h_ref[...]
    g = jnp.dot(h, wg_ref[...], preferred_element_type=F32)
    u = jnp.dot(h, wu_ref[...], preferred_element_type=F32)
    o_ref[...] = (_silu(g) * u).astype(o_ref.dtype)


def ffn_up(h, wg, wu, tm=1024, tn=512):
    t, k = h.shape
    n = wg.shape[1]
    return pl.pallas_call(
        _ffn_up_kernel,
        grid=(t // tm, n // tn),
        in_specs=[
            pl.BlockSpec((tm, k), lambda i, j: (i, 0)),
            pl.BlockSpec((k, tn), lambda i, j: (0, j)),
            pl.BlockSpec((k, tn), lambda i, j: (0, j)),
        ],
        out_specs=pl.BlockSpec((tm, tn), lambda i, j: (i, j)),
        out_shape=jax.ShapeDtypeStruct((t, n), BF16),
        compiler_params=_cparams(("parallel", "arbitrary")),
        name="ffn_up",
    )(h, wg, wu)


def _relayout_w_in(w_in):
    nl, d, _ = w_in.shape
    o_xbc, o_dt, o_scb, o_ik, o_iw, o_gate = 2048, 6144, 6176, 13344, 13408, 13424
    ik = w_in[:, :, o_ik:o_iw]
    parts = [
        w_in[:, :, :o_xbc],
        w_in[:, :, o_gate:],
        w_in[:, :, o_xbc:o_dt],
        w_in[:, :, o_scb:o_ik],
        ik, w_in[:, :, o_dt:o_scb], w_in[:, :, o_iw:o_gate], jnp.zeros((nl, d, 16), w_in.dtype),
        jnp.zeros((nl, d, IDX_DIM), w_in.dtype), ik,
    ]
    return jnp.concatenate(parts, axis=-1).astype(BF16)


def kernel(x, c, w_ada, b_ada, g_mix, w_in, b_gate, ssd_conv_w, ssd_conv_b, ssd_dt_bias, ssd_a_log, ssd_d,
           ssd_norm_g, sc_conv_w, w_br_ssd, w_br_sc, w_br_att, w_out, g_ffn, w_ffn_gate, w_ffn_up, w_ffn_down,
           g_final):
    nb, seq, d = x.shape
    depth = w_in.shape[0]
    t = nb * seq
    assert d == D and w_in.shape[2] == 19568

    mod_all = ada_mod(c, w_ada, b_ada).reshape(depth, nb, 6, d)
    w_in_p = _relayout_w_in(w_in)
    w1, w2, w3 = w_br_ssd.astype(BF16), w_br_sc.astype(BF16), w_br_att.astype(BF16)
    wo, wg, wu, wd = w_out.astype(BF16), w_ffn_gate.astype(BF16), w_ffn_up.astype(BF16), w_ffn_down.astype(BF16)

    x2 = x.reshape(t, d)
    for l in range(depth):
        mod = mod_all[l]
        h = norm_mod(x2, g_mix[l], mod, seq, shift_row=0, scale_row=1)
        proj = matmul_bf16(h, w_in_p[l], tm=1024, tn=1408)
        xbc = ssd_conv(proj, ssd_conv_w[l], ssd_conv_b[l], nb, seq)
        y_ssd = ssd_scan(xbc, proj, ssd_dt_bias[l], ssd_a_log[l], ssd_d[l], ssd_norm_g[l], nb, seq)
        y_sc = short_conv(proj, sc_conv_w[l], nb, seq)
        y_att = dsa_attention(proj, nb, seq)
        merged = merge_branches(y_ssd, y_sc, y_att, proj, b_gate[l], w1[l], w2[l], w3[l])
        x2 = proj_residual(merged, wo[l], x2, mod, seq, gate_row=2, tm=1024, tn=512, name="out_proj")
        h = norm_mod(x2, g_ffn[l], mod, seq, shift_row=3, scale_row=4)
        a = ffn_up(h, wg[l], wu[l])
        x2 = proj_residual(a, wd[l], x2, mod, seq, gate_row=5, tm=512, tn=512, name="ffn_down")
    return final_norm(x2, g_final).reshape(nb, seq, d)
```

```python
import functools
import math

import jax
import jax.numpy as jnp
from jax import lax
from jax.experimental import pallas as pl
from jax.experimental.pallas import tpu as pltpu

F32 = jnp.float32
BF16 = jnp.bfloat16
I32 = jnp.int32

EPS = 1e-6
CHUNK = 64
SSD_HEADDIM = 64
SSD_HEADS = 32
SSD_GROUPS = 8
SSD_STATE = 128
SSD_HEADS_PER_GROUP = SSD_HEADS // SSD_GROUPS
SSD_CONV = 4
SC_CONV = 3
ATT_HEADS = 8
ATT_HEADDIM = 128
IDX_HEADS = 16
IDX_DIM = 64
TOPK_MAX = 256
N_BRANCH = 3

LANES = 128
BF16_SUBLANES = 16
VMEM_LIMIT_BYTES = 56 * 1024 * 1024

D = 2048
COL_Z = 0
COL_GATE = 2048
COL_XBC = 8192
COL_SCB = 12288
COL_SCC = 13312
COL_SCH = 14336
COL_Q = 15360
COL_K = 16384
COL_V = 17408
COL_IQ = 18432
COL_SMALL = 19456
COL_IK_ODD = 19584
N_PROJ = 19712
SMALL_DT = 64
SMALL_IW = 96

NEG_BIG = -1e30
INT_MIN = -2 ** 31
NEG_INF_KEY = -2139095041


def _cparams(sem):
    return pltpu.CompilerParams(dimension_semantics=sem, vmem_limit_bytes=VMEM_LIMIT_BYTES)


def _sigmoid(x):
    return 1.0 / (1.0 + jnp.exp(-x))


def _silu(x):
    return x * _sigmoid(x)


def _ada_kernel(cb_ref, w_ref, b_ref, o_ref, *, nb, tn):
    w = w_ref[...]
    for b in range(nb):
        c = cb_ref[b]
        cs = _silu(c)
        outs = []
        for j in range(tn // LANES):
            prod = w[:, j * LANES:(j + 1) * LANES] * cs
            outs.append(jnp.sum(prod, axis=0, keepdims=True))
        o_ref[b:b + 1, :] = jnp.concatenate(outs, axis=1) + b_ref[...]


def ada_mod(c, w_ada, b_ada, tn=1024):
    nl, d, n6 = w_ada.shape
    nb = c.shape[0]
    cb = jnp.broadcast_to(c[:, :, None], (nb, d, LANES))
    return pl.pallas_call(
        functools.partial(_ada_kernel, nb=nb, tn=tn),
        grid=(nl, n6 // tn),
        in_specs=[
            pl.BlockSpec((nb, d, LANES), lambda l, j: (0, 0, 0)),
            pl.BlockSpec((None, d, tn), lambda l, j: (l, 0, j)),
            pl.BlockSpec((None, 1, tn), lambda l, j: (l, 0, j)),
        ],
        out_specs=pl.BlockSpec((None, nb, tn), lambda l, j: (l, 0, j)),
        out_shape=jax.ShapeDtypeStruct((nl, nb, n6), F32),
        compiler_params=_cparams(("arbitrary", "arbitrary")),
        name="ada_mod",
    )(cb, w_ada, b_ada.reshape(nl, 1, n6))


def _norm_mod_kernel(x_ref, g_ref, mod_ref, o_ref, *, shift_row, scale_row):
    x = x_ref[...]
    y = x * lax.rsqrt(jnp.mean(x * x, axis=-1, keepdims=True) + EPS) * g_ref[...]
    sc = mod_ref[scale_row:scale_row + 1, :]
    sh = mod_ref[shift_row:shift_row + 1, :]
    o_ref[...] = (y * (1.0 + sc) + sh).astype(o_ref.dtype)


def norm_mod(x2, g, mod, seq, shift_row, scale_row, ts=512):
    t, d = x2.shape
    nsb = seq // ts
    return pl.pallas_call(
        functools.partial(_norm_mod_kernel, shift_row=shift_row, scale_row=scale_row),
        grid=(t // ts,),
        in_specs=[
            pl.BlockSpec((ts, d), lambda i: (i, 0)),
            pl.BlockSpec((1, d), lambda i: (0, 0)),
            pl.BlockSpec((None, 6, d), lambda i: (i // nsb, 0, 0)),
        ],
        out_specs=pl.BlockSpec((ts, d), lambda i: (i, 0)),
        out_shape=jax.ShapeDtypeStruct((t, d), BF16),
        compiler_params=_cparams(("parallel",)),
        name="norm_mod",
    )(x2, g.reshape(1, d), mod)


def _final_norm_kernel(x_ref, g_ref, o_ref):
    x = x_ref[...]
    o_ref[...] = x * lax.rsqrt(jnp.mean(x * x, axis=-1, keepdims=True) + EPS) * g_ref[...]


def final_norm(x2, g, ts=512):
    t, d = x2.shape
    return pl.pallas_call(
        _final_norm_kernel,
        grid=(t // ts,),
        in_specs=[pl.BlockSpec((ts, d), lambda i: (i, 0)), pl.BlockSpec((1, d), lambda i: (0, 0))],
        out_specs=pl.BlockSpec((ts, d), lambda i: (i, 0)),
        out_shape=jax.ShapeDtypeStruct((t, d), F32),
        compiler_params=_cparams(("parallel",)),
        name="final_norm",
    )(x2, g.reshape(1, d))


def _mm_kernel(a_ref, w_ref, o_ref):
    o_ref[...] = jnp.dot(a_ref[...], w_ref[...], preferred_element_type=F32).astype(o_ref.dtype)


def matmul_bf16(a, w_all, layer, tm, tn):
    m, k = a.shape
    n = w_all.shape[2]
    return pl.pallas_call(
        _mm_kernel,
        grid=(m // tm, n // tn),
        in_specs=[pl.BlockSpec((tm, k), lambda i, j: (i, 0)),
                  pl.BlockSpec((None, k, tn), lambda i, j: (layer, 0, j))],
        out_specs=pl.BlockSpec((tm, tn), lambda i, j: (i, j)),
        out_shape=jax.ShapeDtypeStruct((m, n), BF16),
        compiler_params=_cparams(("parallel", "arbitrary")),
        name="in_proj",
    )(a, w_all)


HALO = BF16_SUBLANES


def _fill_window(scr, x_ref, halo_ref, first, ts):
    halo = halo_ref[...].astype(F32)
    scr[0:HALO, :] = jnp.where(first, 0.0, halo)
    scr[HALO:HALO + ts, :] = x_ref[...].astype(F32)


def _ssd_conv_kernel(x_ref, halo_ref, w_ref, b_ref, o_ref, scr, *, ts):
    _fill_window(scr, x_ref, halo_ref, pl.program_id(1) == 0, ts)
    acc = jnp.broadcast_to(b_ref[...], (ts, b_ref.shape[1]))
    for k in range(SSD_CONV):
        off = HALO - (SSD_CONV - 1) + k
        acc = acc + w_ref[k:k + 1, :] * scr[off:off + ts, :]
    o_ref[...] = _silu(acc).astype(o_ref.dtype)


def ssd_conv(proj, conv_w, conv_b, nb, seq, ts=512, tc=1024):
    t = proj.shape[0]
    cw = conv_w.shape[1]
    nsb = seq // ts
    col0 = COL_XBC // tc

    def halo_map(b, i, j):
        return (jnp.maximum((b * seq + i * ts) // HALO - 1, 0), col0 + j)

    return pl.pallas_call(
        functools.partial(_ssd_conv_kernel, ts=ts),
        grid=(nb, nsb, cw // tc),
        in_specs=[
            pl.BlockSpec((ts, tc), lambda b, i, j: (b * nsb + i, col0 + j)),
            pl.BlockSpec((HALO, tc), halo_map),
            pl.BlockSpec((SSD_CONV, tc), lambda b, i, j: (0, j)),
            pl.BlockSpec((1, tc), lambda b, i, j: (0, j)),
        ],
        out_specs=pl.BlockSpec((ts, tc), lambda b, i, j: (b * nsb + i, j)),
        out_shape=jax.ShapeDtypeStruct((t, cw), BF16),
        scratch_shapes=[pltpu.VMEM((HALO + ts, tc), F32)],
        compiler_params=_cparams(("parallel", "parallel", "parallel")),
        name="ssd_conv",
    )(proj, proj, conv_w, conv_b.reshape(1, cw))


def _short_conv_kernel(b_ref, c_ref, h_ref, chalo_ref, hhalo_ref, w_ref, o_ref, scr, *, ts):
    first = pl.program_id(1) == 0
    halo = chalo_ref[...].astype(F32) * hhalo_ref[...].astype(F32)
    scr[0:HALO, :] = jnp.where(first, 0.0, halo)
    scr[HALO:HALO + ts, :] = c_ref[...].astype(F32) * h_ref[...].astype(F32)
    acc = None
    for k in range(SC_CONV):
        off = HALO - (SC_CONV - 1) + k
        term = w_ref[k:k + 1, :] * scr[off:off + ts, :]
        acc = term if acc is None else acc + term
    o_ref[...] = (b_ref[...].astype(F32) * acc).astype(o_ref.dtype)


def short_conv(proj, conv_w, nb, seq, ts=512):
    t = proj.shape[0]
    w = conv_w.shape[1]
    nsb = seq // ts
    cb, cc, ch = COL_SCB // w, COL_SCC // w, COL_SCH // w

    def halo_row(b, i):
        return jnp.maximum((b * seq + i * ts) // HALO - 1, 0)

    return pl.pallas_call(
        functools.partial(_short_conv_kernel, ts=ts),
        grid=(nb, nsb),
        in_specs=[
            pl.BlockSpec((ts, w), lambda b, i: (b * nsb + i, cb)),
            pl.BlockSpec((ts, w), lambda b, i: (b * nsb + i, cc)),
            pl.BlockSpec((ts, w), lambda b, i: (b * nsb + i, ch)),
            pl.BlockSpec((HALO, w), lambda b, i: (halo_row(b, i), cc)),
            pl.BlockSpec((HALO, w), lambda b, i: (halo_row(b, i), ch)),
            pl.BlockSpec((SC_CONV, w), lambda b, i: (0, 0)),
        ],
        out_specs=pl.BlockSpec((ts, w), lambda b, i: (b * nsb + i, 0)),
        out_shape=jax.ShapeDtypeStruct((t, w), BF16),
        scratch_shapes=[pltpu.VMEM((HALO + ts, w), F32)],
        compiler_params=_cparams(("parallel", "parallel")),
        name="short_conv",
    )(proj, proj, proj, proj, proj, conv_w)


def _split3(a):
    a1 = a.astype(BF16)
    r1 = a - a1.astype(F32)
    a2 = r1.astype(BF16)
    r2 = r1 - a2.astype(F32)
    return a1, a2, r2.astype(BF16)


def _sel_dot_left(sel, a):
    out = None
    for p in _split3(a):
        term = jnp.dot(sel, p, preferred_element_type=F32)
        out = term if out is None else out + term
    return out


def _sel_dot_right(a, sel):
    out = None
    for p in _split3(a):
        term = jnp.dot(p, sel, preferred_element_type=F32)
        out = term if out is None else out + term
    return out


def _dot_nt(a, b):
    return lax.dot_general(a, b, (((1,), (1,)), ((), ())), preferred_element_type=F32)


def _dot_tn(a, b):
    return lax.dot_general(a, b, (((0,), (0,)), ((), ())), preferred_element_type=F32)


SSD_ROWS = 2 * CHUNK


def _ssd_kernel(xs_ref, b_ref, c_ref, small_ref, z_ref, dtb_ref, alog_ref, dexp_ref, ng_ref, e_ref,
                o_ref, h_scr, y_scr):
    q = CHUNK
    hp = SSD_HEADS_PER_GROUP * SSD_HEADDIM
    width = SSD_HEADS * SSD_HEADDIM

    @pl.when(pl.program_id(1) == 0)
    def _():
        h_scr[...] = jnp.zeros_like(h_scr)

    lane = lax.broadcasted_iota(I32, (1, LANES), 1)
    dt_lane = (lane >= SMALL_DT) & (lane < SMALL_DT + SSD_HEADS)
    a_neg = jnp.where(dt_lane, -jnp.exp(alog_ref[...]), 0.0)

    row = lax.broadcasted_iota(I32, (q, q), 0)
    col = lax.broadcasted_iota(I32, (q, q), 1)
    tril = jnp.where(col <= row, 1.0, 0.0).astype(BF16)
    ones = jnp.ones((q, q), BF16)
    rowt = lax.broadcasted_iota(I32, (q, width), 0)
    colt = lax.broadcasted_iota(I32, (q, width), 1) % q
    eye_t = colt == rowt
    causal_t = colt <= rowt
    rbd = lax.broadcasted_iota(I32, (hp, hp), 0) // q
    cbd = lax.broadcasted_iota(I32, (hp, hp), 1) // SSD_HEADDIM
    bd_mask = rbd == cbd
    e = e_ref[...]

    for ci in range(SSD_ROWS // q):
        r0 = ci * q
        sm = small_ref[r0:r0 + q, :].astype(F32)
        pre = sm + dtb_ref[...]
        dt = jnp.where(dt_lane, jnp.maximum(pre, 0.0) + jnp.log(1.0 + jnp.exp(-jnp.abs(pre))), 0.0)
        a_cs = _sel_dot_left(tril, dt * a_neg)
        a_col = _sel_dot_right(a_cs, e)
        dtx = _sel_dot_right(dt, e)
        a_row = _sel_dot_left(ones, jnp.where(eye_t, a_col, 0.0))
        decay = jnp.exp(jnp.where(causal_t, a_col - a_row, -jnp.inf))
        a_last = a_col[q - 1:q, :]
        to_end = jnp.exp(a_last - a_col)
        from_start = jnp.exp(a_col)
        chunk_decay = jnp.exp(a_last)

        xf = xs_ref[r0:r0 + q, :].astype(F32)
        xdt = xf * dtx
        xdt_b = xdt.astype(BF16)
        xend_b = (xdt * to_end).astype(BF16)

        for g in range(SSD_GROUPS):
            bg = b_ref[r0:r0 + q, g * SSD_STATE:(g + 1) * SSD_STATE]
            cg = c_ref[r0:r0 + q, g * SSD_STATE:(g + 1) * SSD_STATE]
            sl = slice(g * hp, (g + 1) * hp)
            b4 = jnp.concatenate([bg] * SSD_HEADS_PER_GROUP, axis=0)
            cb_t = _dot_nt(cg, b4)
            m_g = (cb_t * decay[:, sl]).astype(BF16)
            xg = xdt_b[:, sl]
            x4 = jnp.concatenate([xg] * SSD_HEADS_PER_GROUP, axis=0)
            x_bd = jnp.where(bd_mask, x4, jnp.zeros_like(x4))
            y_diag = jnp.dot(m_g, x_bd, preferred_element_type=F32)
            h_g = h_scr[g]
            y_off = jnp.dot(cg, h_g.astype(BF16), preferred_element_type=F32) * from_start[:, sl]
            h_scr[g] = h_g * chunk_decay[:, sl] + _dot_tn(bg, xend_b[:, sl])
            y_scr[r0:r0 + q, sl] = y_diag + y_off

        y = y_scr[r0:r0 + q, :] + dexp_ref[...] * xf
        gated = y * _silu(z_ref[r0:r0 + q, :].astype(F32))
        out = gated * lax.rsqrt(jnp.mean(gated * gated, axis=-1, keepdims=True) + EPS) * ng_ref[...]
        o_ref[r0:r0 + q, :] = out.astype(o_ref.dtype)


def ssd_scan(xbc, proj, dt_bias, a_log, d_skip, norm_g, nb, seq):
    t = proj.shape[0]
    width = SSD_HEADS * SSD_HEADDIM
    gw = SSD_GROUPS * SSD_STATE
    rows = SSD_ROWS
    nsb = seq // rows

    def pad_small(v):
        return jnp.zeros((1, LANES), F32).at[0, SMALL_DT:SMALL_DT + SSD_HEADS].set(v.astype(F32))

    h_idx = jnp.arange(LANES)[:, None] - SMALL_DT
    c_idx = jnp.arange(width)[None, :] // SSD_HEADDIM
    e = (h_idx == c_idx).astype(BF16)
    dexp = jnp.repeat(d_skip.astype(F32), SSD_HEADDIM).reshape(1, width)

    row_map = lambda b, i: (b * nsb + i, 0)
    const = lambda b, i: (0, 0)
    return pl.pallas_call(
        _ssd_kernel,
        grid=(nb, nsb),
        in_specs=[
            pl.BlockSpec((rows, width), row_map),
            pl.BlockSpec((rows, gw), lambda b, i: (b * nsb + i, width // gw)),
            pl.BlockSpec((rows, gw), lambda b, i: (b * nsb + i, width // gw + 1)),
            pl.BlockSpec((rows, LANES), lambda b, i: (b * nsb + i, COL_SMALL // LANES)),
            pl.BlockSpec((rows, width), lambda b, i: (b * nsb + i, COL_Z // width)),
            pl.BlockSpec((1, LANES), const),
            pl.BlockSpec((1, LANES), const),
            pl.BlockSpec((1, width), const),
            pl.BlockSpec((1, width), const),
            pl.BlockSpec((LANES, width), const),
        ],
        out_specs=pl.BlockSpec((rows, width), row_map),
        out_shape=jax.ShapeDtypeStruct((t, width), BF16),
        scratch_shapes=[
            pltpu.VMEM((SSD_GROUPS, SSD_STATE, SSD_HEADS_PER_GROUP * SSD_HEADDIM), F32),
            pltpu.VMEM((rows, width), F32),
        ],
        compiler_params=_cparams(("parallel", "arbitrary")),
        name="ssd_scan",
    )(xbc, xbc, xbc, proj, proj, pad_small(dt_bias), pad_small(a_log), dexp,
      norm_g.reshape(1, width).astype(F32), e)


def _float_key(s):
    bits = pltpu.bitcast(s, I32)
    return bits ^ ((bits >> 31) & 0x7FFFFFFF)


def _dsa_kernel(q_ref, k_ref, v_ref, iq_ref, smq_ref, smk_ref, iko_ref, o_ref,
                key_scr, wb_scr, m_scr, l_scr, acc_scr, j_scr, *, tq, tk, seq, topk):
    qi = pl.program_id(1)
    q0 = qi * tq
    kend = q0 + tq
    nt = (kend + tk - 1) // tk
    idx_bits = max(1, (seq - 1).bit_length())

    smq = smq_ref[...].astype(F32)
    for h in range(IDX_HEADS):
        wb_scr[h] = jnp.broadcast_to(smq[:, SMALL_IW + h:SMALL_IW + h + 1], (tq, LANES))
    lane = lax.broadcasted_iota(I32, (tk, LANES), 1)
    q_chunk = (q0 + lax.broadcasted_iota(I32, (tq, tk), 0)) // CHUNK
    k_local = lax.broadcasted_iota(I32, (tq, tk), 1)
    nrep = tk // LANES

    def score_tile(kt, carry):
        k0 = pl.multiple_of(kt * tk, tk)
        smk = smk_ref[pl.ds(k0, tk), :]
        ik_even = jnp.where(lane < IDX_DIM, smk, jnp.zeros_like(smk))
        ik_odd = iko_ref[pl.ds(k0, tk), :]
        acc = jnp.zeros((tq, tk), F32)
        for hp in range(IDX_HEADS // 2):
            iq2 = iq_ref[:, hp * LANES:(hp + 1) * LANES]
            le = jnp.maximum(_dot_nt(iq2, ik_even), 0.0)
            lo = jnp.maximum(_dot_nt(iq2, ik_odd), 0.0)
            we = jnp.concatenate([wb_scr[2 * hp]] * nrep, axis=1)
            wo = jnp.concatenate([wb_scr[2 * hp + 1]] * nrep, axis=1)
            acc = acc + we * le + wo * lo
        admissible = (k0 + k_local) // CHUNK <= q_chunk
        key_scr[:, pl.ds(k0, tk)] = jnp.where(admissible, _float_key(acc), NEG_INF_KEY)
        return carry

    lax.fori_loop(0, nt, score_tile, 0)

    def count_rows(pred):
        def body(kt, cnt):
            k0 = pl.multiple_of(kt * tk, tk)
            hit = pred(key_scr[:, pl.ds(k0, tk)], k0)
            for j in range(nrep):
                cnt = cnt + jnp.where(hit[:, j * LANES:(j + 1) * LANES], 1, 0)
            return cnt
        cnt = lax.fori_loop(0, nt, body, jnp.zeros((tq, LANES), I32))
        return jnp.sum(cnt, axis=1, keepdims=True)

    def radix_step(it, t_u):
        c_u = t_u | lax.shift_left(jnp.int32(1), 31 - it)
        c_s = c_u ^ INT_MIN
        cnt = count_rows(lambda keys, k0: keys >= c_s)
        return jnp.where(cnt >= topk, c_u, t_u)

    t_u = lax.fori_loop(0, 32, radix_step, jnp.zeros((tq, 1), I32))
    thr = t_u ^ INT_MIN

    n_gt = count_rows(lambda keys, k0: keys > thr)
    n_eq = count_rows(lambda keys, k0: keys == thr)
    need = topk - n_gt
    excess = (n_eq > need) & (thr > NEG_INF_KEY)
    j_scr[...] = jnp.full((tq, 1), seq, I32)

    @pl.when(jnp.max(jnp.where(excess, 1, 0)) > 0)
    def _():
        def idx_step(it, jb):
            c = jb | lax.shift_left(jnp.int32(1), idx_bits - 1 - it)
            cnt = count_rows(lambda keys, k0: (keys == thr) & ((k0 + k_local) < c))
            return jnp.where(cnt < need, c, jb)
        jb = lax.fori_loop(0, idx_bits, idx_step, jnp.zeros((tq, 1), I32))
        j_scr[...] = jnp.where(excess, jb, seq)

    j_lim = j_scr[...]

    m_scr[...] = jnp.full(m_scr.shape, NEG_BIG, F32)
    l_scr[...] = jnp.zeros_like(l_scr)
    acc_scr[...] = jnp.zeros_like(acc_scr)
    scale = ATT_HEADDIM ** -0.5

    def attn_tile(kt, carry):
        k0 = pl.multiple_of(kt * tk, tk)
        keys = key_scr[:, pl.ds(k0, tk)]
        sel = (keys > thr) | ((keys == thr) & ((k0 + k_local) <= j_lim))
        sel = sel & (keys > NEG_INF_KEY)
        for h in range(ATT_HEADS):
            hs = slice(h * ATT_HEADDIM, (h + 1) * ATT_HEADDIM)
            qh = (q_ref[:, hs].astype(F32) * scale).astype(BF16)
            s = _dot_nt(qh, k_ref[pl.ds(k0, tk), hs])
            s = jnp.where(sel, s, NEG_BIG)
            m_old = m_scr[h]
            m_new = jnp.maximum(m_old, jnp.max(s, axis=1, keepdims=True))
            alpha = jnp.exp(m_old - m_new)
            p = jnp.exp(s - m_new)
            l_scr[h] = alpha * l_scr[h] + jnp.sum(p, axis=1, keepdims=True)
            pv = jnp.dot(p.astype(BF16), v_ref[pl.ds(k0, tk), hs], preferred_element_type=F32)
            acc_scr[:, hs] = alpha * acc_scr[:, hs] + pv
            m_scr[h] = m_new
        return carry

    lax.fori_loop(0, nt, attn_tile, 0)
    for h in range(ATT_HEADS):
        hs = slice(h * ATT_HEADDIM, (h + 1) * ATT_HEADDIM)
        o_ref[:, hs] = (acc_scr[:, hs] / l_scr[h]).astype(o_ref.dtype)


def dsa_attention(proj, nb, seq, tq=128, tk=512):
    t = proj.shape[0]
    w = ATT_HEADS * ATT_HEADDIM
    tk = min(tk, seq)
    nqb = seq // tq
    topk = min(TOPK_MAX, seq // 4)
    qrow = lambda b, i: b * nqb + i
    return pl.pallas_call(
        functools.partial(_dsa_kernel, tq=tq, tk=tk, seq=seq, topk=topk),
        grid=(nb, nqb),
        in_specs=[
            pl.BlockSpec((tq, w), lambda b, i: (qrow(b, i), COL_Q // w)),
            pl.BlockSpec((seq, w), lambda b, i: (b, COL_K // w)),
            pl.BlockSpec((seq, w), lambda b, i: (b, COL_V // w)),
            pl.BlockSpec((tq, w), lambda b, i: (qrow(b, i), COL_IQ // w)),
            pl.BlockSpec((tq, LANES), lambda b, i: (qrow(b, i), COL_SMALL // LANES)),
            pl.BlockSpec((seq, LANES), lambda b, i: (b, COL_SMALL // LANES)),
            pl.BlockSpec((seq, LANES), lambda b, i: (b, COL_IK_ODD // LANES)),
        ],
        out_specs=pl.BlockSpec((tq, w), lambda b, i: (qrow(b, i), 0)),
        out_shape=jax.ShapeDtypeStruct((t, w), BF16),
        scratch_shapes=[
            pltpu.VMEM((tq, seq), I32),
            pltpu.VMEM((IDX_HEADS, tq, LANES), F32),
            pltpu.VMEM((ATT_HEADS, tq, 1), F32),
            pltpu.VMEM((ATT_HEADS, tq, 1), F32),
            pltpu.VMEM((tq, w), F32),
            pltpu.VMEM((tq, 1), I32),
        ],
        compiler_params=_cparams(("parallel", "arbitrary")),
        name="dsa",
    )(proj, proj, proj, proj, proj, proj, proj)


def _cast_weights_once(pairs):
    @pl.when(pl.program_id(1) == 0)
    def _():
        for w_ref, s_ref in pairs:
            s_ref[...] = w_ref[...].astype(BF16)


def _merge_kernel(ys_ref, yc_ref, ya_ref, g0_ref, g1_ref, g2_ref, bg_ref, w1_ref, w2_ref, w3_ref, o_ref,
                  s1, s2, s3):
    _cast_weights_once(((w1_ref, s1), (w2_ref, s2), (w3_ref, s3)))

    def gate(g_ref, r):
        return _sigmoid(g_ref[...].astype(F32) + bg_ref[r:r + 1, :])

    acc = gate(g0_ref, 0) * jnp.dot(ys_ref[...], s1[...], preferred_element_type=F32)
    acc = acc + gate(g1_ref, 1) * jnp.dot(yc_ref[...], s2[...], preferred_element_type=F32)
    acc = acc + gate(g2_ref, 2) * jnp.dot(ya_ref[...], s3[...], preferred_element_type=F32)
    o_ref[...] = acc.astype(o_ref.dtype)


def merge_branches(y_ssd, y_sc, y_att, proj, b_gate, w1, w2, w3, layer, tm=1024, tn=512):
    t, d = y_ssd.shape[0], w1.shape[2]
    gcol = COL_GATE // tn
    gstep = d // tn
    wspec = lambda w: pl.BlockSpec((None, w.shape[1], tn), lambda j, i: (layer, 0, j))
    return pl.pallas_call(
        _merge_kernel,
        grid=(d // tn, t // tm),
        in_specs=[
            pl.BlockSpec((tm, y_ssd.shape[1]), lambda j, i: (i, 0)),
            pl.BlockSpec((tm, y_sc.shape[1]), lambda j, i: (i, 0)),
            pl.BlockSpec((tm, y_att.shape[1]), lambda j, i: (i, 0)),
            pl.BlockSpec((tm, tn), lambda j, i: (i, gcol + j)),
            pl.BlockSpec((tm, tn), lambda j, i: (i, gcol + gstep + j)),
            pl.BlockSpec((tm, tn), lambda j, i: (i, gcol + 2 * gstep + j)),
            pl.BlockSpec((None, N_BRANCH, tn), lambda j, i: (layer, 0, j)),
            wspec(w1), wspec(w2), wspec(w3),
        ],
        out_specs=pl.BlockSpec((tm, tn), lambda j, i: (i, j)),
        out_shape=jax.ShapeDtypeStruct((t, d), BF16),
        scratch_shapes=[pltpu.VMEM((w.shape[1], tn), BF16) for w in (w1, w2, w3)],
        compiler_params=_cparams(("arbitrary", "arbitrary")),
        name="merge",
    )(y_ssd, y_sc, y_att, proj, proj, proj, b_gate, w1, w2, w3)


def _proj_residual_kernel(a_ref, w_ref, x_ref, mod_ref, o_ref, ws, *, gate_row):
    _cast_weights_once(((w_ref, ws),))
    y = jnp.dot(a_ref[...], ws[...], preferred_element_type=F32)
    o_ref[...] = x_ref[...] + mod_ref[gate_row:gate_row + 1, :] * y


def proj_residual(a, w_all, layer, x2, mod, seq, gate_row, tm, tn, name):
    t, k = a.shape
    d = w_all.shape[2]
    nsb = seq // tm
    return pl.pallas_call(
        functools.partial(_proj_residual_kernel, gate_row=gate_row),
        grid=(d // tn, t // tm),
        in_specs=[
            pl.BlockSpec((tm, k), lambda j, i: (i, 0)),
            pl.BlockSpec((None, k, tn), lambda j, i: (layer, 0, j)),
            pl.BlockSpec((tm, tn), lambda j, i: (i, j)),
            pl.BlockSpec((None, 6, tn), lambda j, i: (i // nsb, 0, j)),
        ],
        out_specs=pl.BlockSpec((tm, tn), lambda j, i: (i, j)),
        out_shape=jax.ShapeDtypeStruct((t, d), F32),
        scratch_shapes=[pltpu.VMEM((k, tn), BF16)],
        compiler_params=_cparams(("arbitrary", "arbitrary")),
        name=name,
    )(a, w_all, x2, mod)


def _ffn_up_kernel(h_ref, wg_ref, wu_ref, o_ref, sg, su):
    _cast_weights_once(((wg_ref, sg), (wu_ref, su)))
    h = h_ref[...]
    g = jnp.dot(h, sg[...], preferred_element_type=F32)
    u = jnp.dot(h, su[...], preferred_element_type=F32)
    o_ref[...] = (_silu(g) * u).astype(o_ref.dtype)


def ffn_up(h, wg_all, wu_all, layer, tm=1024, tn=512):
    t, k = h.shape
    n = wg_all.shape[2]
    wspec = pl.BlockSpec((None, k, tn), lambda j, i: (layer, 0, j))
    return pl.pallas_call(
        _ffn_up_kernel,
        grid=(n // tn, t // tm),
        in_specs=[pl.BlockSpec((tm, k), lambda j, i: (i, 0)), wspec, wspec],
        out_specs=pl.BlockSpec((tm, tn), lambda j, i: (i, j)),
        out_shape=jax.ShapeDtypeStruct((t, n), BF16),
        scratch_shapes=[pltpu.VMEM((k, tn), BF16), pltpu.VMEM((k, tn), BF16)],
        compiler_params=_cparams(("arbitrary", "arbitrary")),
        name="ffn_up",
    )(h, wg_all, wu_all)


W_IN_COLS = 19568
_W_IN_SEGMENTS = (
    (COL_Z, 0, 2048),
    (COL_GATE, 13424, 6144),
    (COL_XBC, 2048, 4096),
    (COL_SCB, 6176, 7168),
    (COL_SMALL, 13344, IDX_DIM),
    (COL_SMALL + SMALL_DT, 6144, SSD_HEADS),
    (COL_SMALL + SMALL_IW, 13408, IDX_HEADS),
    (COL_IK_ODD + IDX_DIM, 13344, IDX_DIM),
)
_W_IN_ZERO = ((COL_SMALL + SMALL_IW + IDX_HEADS, 16), (COL_IK_ODD, IDX_DIM))
_PREP_CHUNK = 1024


def _prep_w_in_kernel(w_ref, o_ref):
    for dst, src, width in _W_IN_SEGMENTS:
        for c in range(0, width, _PREP_CHUNK):
            n = min(_PREP_CHUNK, width - c)
            o_ref[:, dst + c:dst + c + n] = w_ref[:, src + c:src + c + n].astype(BF16)
    for dst, width in _W_IN_ZERO:
        o_ref[:, dst:dst + width] = jnp.zeros((o_ref.shape[0], width), BF16)


def prep_w_in(w_in, tk=128):
    nl, d, n = w_in.shape
    return pl.pallas_call(
        _prep_w_in_kernel,
        grid=(nl, d // tk),
        in_specs=[pl.BlockSpec((None, tk, n), lambda l, i: (l, i, 0))],
        out_specs=pl.BlockSpec((None, tk, N_PROJ), lambda l, i: (l, i, 0)),
        out_shape=jax.ShapeDtypeStruct((nl, d, N_PROJ), BF16),
        compiler_params=_cparams(("parallel", "parallel")),
        name="prep_w_in",
    )(w_in)


def kernel(x, c, w_ada, b_ada, g_mix, w_in, b_gate, ssd_conv_w, ssd_conv_b, ssd_dt_bias, ssd_a_log, ssd_d,
           ssd_norm_g, sc_conv_w, w_br_ssd, w_br_sc, w_br_att, w_out, g_ffn, w_ffn_gate, w_ffn_up, w_ffn_down,
           g_final):
    nb, seq, d = x.shape
    depth = w_in.shape[0]
    t = nb * seq
    assert d == D and w_in.shape[2] == W_IN_COLS

    mod_all = ada_mod(c, w_ada, b_ada).reshape(depth, nb, 6, d)
    w_in_p = prep_w_in(w_in)

    x2 = x.reshape(t, d)
    for l in range(depth):
        mod = mod_all[l]
        h = norm_mod(x2, g_mix[l], mod, seq, shift_row=0, scale_row=1)
        proj = matmul_bf16(h, w_in_p, l, tm=1024, tn=1408)
        xbc = ssd_conv(proj, ssd_conv_w[l], ssd_conv_b[l], nb, seq)
        y_ssd = ssd_scan(xbc, proj, ssd_dt_bias[l], ssd_a_log[l], ssd_d[l], ssd_norm_g[l], nb, seq)
        y_sc = short_conv(proj, sc_conv_w[l], nb, seq)
        y_att = dsa_attention(proj, nb, seq)
        merged = merge_branches(y_ssd, y_sc, y_att, proj, b_gate, w_br_ssd, w_br_sc, w_br_att, l)
        x2 = proj_residual(merged, w_out, l, x2, mod, seq, gate_row=2, tm=1024, tn=512, name="out_proj")
        h = norm_mod(x2, g_ffn[l], mod, seq, shift_row=3, scale_row=4)
        a = ffn_up(h, w_ffn_gate, w_ffn_up, l)
        x2 = proj_residual(a, w_ffn_down, l, x2, mod, seq, gate_row=5, tm=512, tn=512, name="ffn_down")
    return final_norm(x2, g_final).reshape(nb, seq, d)
```

```python
import functools
import math

import jax
import jax.numpy as jnp
from jax import lax
from jax.experimental import pallas as pl
from jax.experimental.pallas import tpu as pltpu

F32 = jnp.float32
BF16 = jnp.bfloat16
I32 = jnp.int32

EPS = 1e-6
CHUNK = 64
SSD_HEADDIM = 64
SSD_HEADS = 32
SSD_GROUPS = 8
SSD_STATE = 128
SSD_HEADS_PER_GROUP = SSD_HEADS // SSD_GROUPS
SSD_CONV = 4
SC_CONV = 3
ATT_HEADS = 8
ATT_HEADDIM = 128
IDX_HEADS = 16
IDX_DIM = 64
TOPK_MAX = 256
N_BRANCH = 3

LANES = 128
BF16_SUBLANES = 16
VMEM_LIMIT_BYTES = 56 * 1024 * 1024

D = 2048
COL_Z = 0
COL_GATE = 2048
COL_XBC = 8192
COL_SCB = 12288
COL_SCC = 13312
COL_SCH = 14336
COL_Q = 15360
COL_K = 16384
COL_V = 17408
COL_IQ = 18432
COL_SMALL = 19456
COL_IK_ODD = 19584
N_PROJ = 19712
SMALL_DT = 64
SMALL_IW = 96

NEG_BIG = -1e30
INT_MIN = -2 ** 31
NEG_INF_KEY = -2139095041


def _cparams(sem):
    return pltpu.CompilerParams(dimension_semantics=sem, vmem_limit_bytes=VMEM_LIMIT_BYTES)


def _sigmoid(x):
    return 1.0 / (1.0 + jnp.exp(-x))


def _silu(x):
    return x * _sigmoid(x)


def _ada_kernel(cb_ref, w_ref, b_ref, o_ref, *, nb, tn):
    w = w_ref[...]
    for b in range(nb):
        c = cb_ref[b]
        cs = _silu(c)
        outs = []
        for j in range(tn // LANES):
            prod = w[:, j * LANES:(j + 1) * LANES] * cs
            outs.append(jnp.sum(prod, axis=0, keepdims=True))
        o_ref[b:b + 1, :] = jnp.concatenate(outs, axis=1) + b_ref[...]


def ada_mod(c, w_ada, b_ada, tn=1024):
    nl, d, n6 = w_ada.shape
    nb = c.shape[0]
    cb = jnp.broadcast_to(c[:, :, None], (nb, d, LANES))
    return pl.pallas_call(
        functools.partial(_ada_kernel, nb=nb, tn=tn),
        grid=(nl, n6 // tn),
        in_specs=[
            pl.BlockSpec((nb, d, LANES), lambda l, j: (0, 0, 0)),
            pl.BlockSpec((None, d, tn), lambda l, j: (l, 0, j)),
            pl.BlockSpec((None, 1, tn), lambda l, j: (l, 0, j)),
        ],
        out_specs=pl.BlockSpec((None, nb, tn), lambda l, j: (l, 0, j)),
        out_shape=jax.ShapeDtypeStruct((nl, nb, n6), F32),
        compiler_params=_cparams(("arbitrary", "arbitrary")),
        name="ada_mod",
    )(cb, w_ada, b_ada.reshape(nl, 1, n6))


def _norm_mod_kernel(x_ref, g_ref, mod_ref, o_ref, *, shift_row, scale_row):
    x = x_ref[...]
    y = x * lax.rsqrt(jnp.mean(x * x, axis=-1, keepdims=True) + EPS) * g_ref[...]
    sc = mod_ref[scale_row:scale_row + 1, :]
    sh = mod_ref[shift_row:shift_row + 1, :]
    o_ref[...] = (y * (1.0 + sc) + sh).astype(o_ref.dtype)


def norm_mod(x2, g, mod, seq, shift_row, scale_row, ts=512):
    t, d = x2.shape
    nsb = seq // ts
    return pl.pallas_call(
        functools.partial(_norm_mod_kernel, shift_row=shift_row, scale_row=scale_row),
        grid=(t // ts,),
        in_specs=[
            pl.BlockSpec((ts, d), lambda i: (i, 0)),
            pl.BlockSpec((1, d), lambda i: (0, 0)),
            pl.BlockSpec((None, 6, d), lambda i: (i // nsb, 0, 0)),
        ],
        out_specs=pl.BlockSpec((ts, d), lambda i: (i, 0)),
        out_shape=jax.ShapeDtypeStruct((t, d), BF16),
        compiler_params=_cparams(("parallel",)),
        name="norm_mod",
    )(x2, g.reshape(1, d), mod)


def _final_norm_kernel(x_ref, g_ref, o_ref):
    x = x_ref[...]
    o_ref[...] = x * lax.rsqrt(jnp.mean(x * x, axis=-1, keepdims=True) + EPS) * g_ref[...]


def final_norm(x2, g, ts=512):
    t, d = x2.shape
    return pl.pallas_call(
        _final_norm_kernel,
        grid=(t // ts,),
        in_specs=[pl.BlockSpec((ts, d), lambda i: (i, 0)), pl.BlockSpec((1, d), lambda i: (0, 0))],
        out_specs=pl.BlockSpec((ts, d), lambda i: (i, 0)),
        out_shape=jax.ShapeDtypeStruct((t, d), F32),
        compiler_params=_cparams(("parallel",)),
        name="final_norm",
    )(x2, g.reshape(1, d))


def _mm_kernel(a_ref, w_ref, o_ref):
    o_ref[...] = jnp.dot(a_ref[...], w_ref[...], preferred_element_type=F32).astype(o_ref.dtype)


def matmul_bf16(a, w_all, layer, tm, tn):
    m, k = a.shape
    n = w_all.shape[2]
    return pl.pallas_call(
        _mm_kernel,
        grid=(m // tm, n // tn),
        in_specs=[pl.BlockSpec((tm, k), lambda i, j: (i, 0)),
                  pl.BlockSpec((None, k, tn), lambda i, j: (layer, 0, j))],
        out_specs=pl.BlockSpec((tm, tn), lambda i, j: (i, j)),
        out_shape=jax.ShapeDtypeStruct((m, n), BF16),
        compiler_params=_cparams(("parallel", "arbitrary")),
        name="in_proj",
    )(a, w_all)


HALO = BF16_SUBLANES


def _fill_window(scr, x_ref, halo_ref, first, ts):
    halo = halo_ref[...].astype(F32)
    scr[0:HALO, :] = jnp.where(first, 0.0, halo)
    scr[HALO:HALO + ts, :] = x_ref[...].astype(F32)


def _ssd_conv_kernel(x_ref, halo_ref, w_ref, b_ref, o_ref, scr, *, ts):
    _fill_window(scr, x_ref, halo_ref, pl.program_id(1) == 0, ts)
    acc = jnp.broadcast_to(b_ref[...], (ts, b_ref.shape[1]))
    for k in range(SSD_CONV):
        off = HALO - (SSD_CONV - 1) + k
        acc = acc + w_ref[k:k + 1, :] * scr[off:off + ts, :]
    o_ref[...] = _silu(acc).astype(o_ref.dtype)


def ssd_conv(proj, conv_w, conv_b, nb, seq, ts=512, tc=1024):
    t = proj.shape[0]
    cw = conv_w.shape[1]
    nsb = seq // ts
    col0 = COL_XBC // tc

    def halo_map(b, i, j):
        return (jnp.maximum((b * seq + i * ts) // HALO - 1, 0), col0 + j)

    return pl.pallas_call(
        functools.partial(_ssd_conv_kernel, ts=ts),
        grid=(nb, nsb, cw // tc),
        in_specs=[
            pl.BlockSpec((ts, tc), lambda b, i, j: (b * nsb + i, col0 + j)),
            pl.BlockSpec((HALO, tc), halo_map),
            pl.BlockSpec((SSD_CONV, tc), lambda b, i, j: (0, j)),
            pl.BlockSpec((1, tc), lambda b, i, j: (0, j)),
        ],
        out_specs=pl.BlockSpec((ts, tc), lambda b, i, j: (b * nsb + i, j)),
        out_shape=jax.ShapeDtypeStruct((t, cw), BF16),
        scratch_shapes=[pltpu.VMEM((HALO + ts, tc), F32)],
        compiler_params=_cparams(("parallel", "parallel", "parallel")),
        name="ssd_conv",
    )(proj, proj, conv_w, conv_b.reshape(1, cw))


def _short_conv_kernel(b_ref, c_ref, h_ref, chalo_ref, hhalo_ref, w_ref, o_ref, scr, *, ts):
    first = pl.program_id(1) == 0
    halo = chalo_ref[...].astype(F32) * hhalo_ref[...].astype(F32)
    scr[0:HALO, :] = jnp.where(first, 0.0, halo)
    scr[HALO:HALO + ts, :] = c_ref[...].astype(F32) * h_ref[...].astype(F32)
    acc = None
    for k in range(SC_CONV):
        off = HALO - (SC_CONV - 1) + k
        term = w_ref[k:k + 1, :] * scr[off:off + ts, :]
        acc = term if acc is None else acc + term
    o_ref[...] = (b_ref[...].astype(F32) * acc).astype(o_ref.dtype)


def short_conv(proj, conv_w, nb, seq, ts=512):
    t = proj.shape[0]
    w = conv_w.shape[1]
    nsb = seq // ts
    cb, cc, ch = COL_SCB // w, COL_SCC // w, COL_SCH // w

    def halo_row(b, i):
        return jnp.maximum((b * seq + i * ts) // HALO - 1, 0)

    return pl.pallas_call(
        functools.partial(_short_conv_kernel, ts=ts),
        grid=(nb, nsb),
        in_specs=[
            pl.BlockSpec((ts, w), lambda b, i: (b * nsb + i, cb)),
            pl.BlockSpec((ts, w), lambda b, i: (b * nsb + i, cc)),
            pl.BlockSpec((ts, w), lambda b, i: (b * nsb + i, ch)),
            pl.BlockSpec((HALO, w), lambda b, i: (halo_row(b, i), cc)),
            pl.BlockSpec((HALO, w), lambda b, i: (halo_row(b, i), ch)),
            pl.BlockSpec((SC_CONV, w), lambda b, i: (0, 0)),
        ],
        out_specs=pl.BlockSpec((ts, w), lambda b, i: (b * nsb + i, 0)),
        out_shape=jax.ShapeDtypeStruct((t, w), BF16),
        scratch_shapes=[pltpu.VMEM((HALO + ts, w), F32)],
        compiler_params=_cparams(("parallel", "parallel")),
        name="short_conv",
    )(proj, proj, proj, proj, proj, conv_w)


def _split3(a):
    a1 = a.astype(BF16)
    r1 = a - a1.astype(F32)
    a2 = r1.astype(BF16)
    r2 = r1 - a2.astype(F32)
    return a1, a2, r2.astype(BF16)


def _sel_dot_left(sel, a):
    out = None
    for p in _split3(a):
        term = jnp.dot(sel, p, preferred_element_type=F32)
        out = term if out is None else out + term
    return out


def _sel_dot_right(a, sel):
    out = None
    for p in _split3(a):
        term = jnp.dot(p, sel, preferred_element_type=F32)
        out = term if out is None else out + term
    return out


def _dot_nt(a, b):
    return lax.dot_general(a, b, (((1,), (1,)), ((), ())), preferred_element_type=F32)


def _dot_tn(a, b):
    return lax.dot_general(a, b, (((0,), (0,)), ((), ())), preferred_element_type=F32)


SSD_ROWS = 2 * CHUNK


def _ssd_kernel(xs_ref, b_ref, c_ref, small_ref, z_ref, dtb_ref, alog_ref, dexp_ref, ng_ref, e_ref,
                o_ref, h_scr, y_scr):
    q = CHUNK
    hp = SSD_HEADS_PER_GROUP * SSD_HEADDIM
    width = SSD_HEADS * SSD_HEADDIM

    @pl.when(pl.program_id(1) == 0)
    def _():
        h_scr[...] = jnp.zeros_like(h_scr)

    lane = lax.broadcasted_iota(I32, (1, LANES), 1)
    dt_lane = (lane >= SMALL_DT) & (lane < SMALL_DT + SSD_HEADS)
    a_neg = jnp.where(dt_lane, -jnp.exp(alog_ref[...]), 0.0)

    row = lax.broadcasted_iota(I32, (q, q), 0)
    col = lax.broadcasted_iota(I32, (q, q), 1)
    tril = jnp.where(col <= row, 1.0, 0.0).astype(BF16)
    ones = jnp.ones((q, q), BF16)
    rowt = lax.broadcasted_iota(I32, (q, width), 0)
    colt = lax.broadcasted_iota(I32, (q, width), 1) % q
    eye_t = colt == rowt
    causal_t = colt <= rowt
    rbd = lax.broadcasted_iota(I32, (hp, hp), 0) // q
    cbd = lax.broadcasted_iota(I32, (hp, hp), 1) // SSD_HEADDIM
    bd_mask = rbd == cbd
    e = e_ref[...]

    for ci in range(SSD_ROWS // q):
        r0 = ci * q
        sm = small_ref[r0:r0 + q, :].astype(F32)
        pre = sm + dtb_ref[...]
        dt = jnp.where(dt_lane, jnp.maximum(pre, 0.0) + jnp.log(1.0 + jnp.exp(-jnp.abs(pre))), 0.0)
        a_cs = _sel_dot_left(tril, dt * a_neg)
        a_col = _sel_dot_right(a_cs, e)
        dtx = _sel_dot_right(dt, e)
        a_row = _sel_dot_left(ones, jnp.where(eye_t, a_col, 0.0))
        decay = jnp.exp(jnp.where(causal_t, a_col - a_row, -jnp.inf))
        a_last = a_col[q - 1:q, :]
        to_end = jnp.exp(a_last - a_col)
        from_start = jnp.exp(a_col)
        chunk_decay = jnp.exp(a_last)

        xf = xs_ref[r0:r0 + q, :].astype(F32)
        xdt = xf * dtx
        xdt_b = xdt.astype(BF16)
        xend_b = (xdt * to_end).astype(BF16)

        for g in range(SSD_GROUPS):
            bg = b_ref[r0:r0 + q, g * SSD_STATE:(g + 1) * SSD_STATE]
            cg = c_ref[r0:r0 + q, g * SSD_STATE:(g + 1) * SSD_STATE]
            sl = slice(g * hp, (g + 1) * hp)
            b4 = jnp.concatenate([bg] * SSD_HEADS_PER_GROUP, axis=0)
            cb_t = _dot_nt(cg, b4)
            m_g = (cb_t * decay[:, sl]).astype(BF16)
            xg = xdt_b[:, sl]
            x4 = jnp.concatenate([xg] * SSD_HEADS_PER_GROUP, axis=0)
            x_bd = jnp.where(bd_mask, x4, jnp.zeros_like(x4))
            y_diag = jnp.dot(m_g, x_bd, preferred_element_type=F32)
            h_g = h_scr[g]
            y_off = jnp.dot(cg, h_g.astype(BF16), preferred_element_type=F32) * from_start[:, sl]
            h_scr[g] = h_g * chunk_decay[:, sl] + _dot_tn(bg, xend_b[:, sl])
            y_scr[r0:r0 + q, sl] = y_diag + y_off

        y = y_scr[r0:r0 + q, :] + dexp_ref[...] * xf
        gated = y * _silu(z_ref[r0:r0 + q, :].astype(F32))
        out = gated * lax.rsqrt(jnp.mean(gated * gated, axis=-1, keepdims=True) + EPS) * ng_ref[...]
        o_ref[r0:r0 + q, :] = out.astype(o_ref.dtype)


def ssd_scan(xbc, proj, dt_bias, a_log, d_skip, norm_g, nb, seq):
    t = proj.shape[0]
    width = SSD_HEADS * SSD_HEADDIM
    gw = SSD_GROUPS * SSD_STATE
    rows = SSD_ROWS
    nsb = seq // rows

    def pad_small(v):
        return jnp.zeros((1, LANES), F32).at[0, SMALL_DT:SMALL_DT + SSD_HEADS].set(v.astype(F32))

    h_idx = jnp.arange(LANES)[:, None] - SMALL_DT
    c_idx = jnp.arange(width)[None, :] // SSD_HEADDIM
    e = (h_idx == c_idx).astype(BF16)
    dexp = jnp.repeat(d_skip.astype(F32), SSD_HEADDIM).reshape(1, width)

    row_map = lambda b, i: (b * nsb + i, 0)
    const = lambda b, i: (0, 0)
    return pl.pallas_call(
        _ssd_kernel,
        grid=(nb, nsb),
        in_specs=[
            pl.BlockSpec((rows, width), row_map),
            pl.BlockSpec((rows, gw), lambda b, i: (b * nsb + i, width // gw)),
            pl.BlockSpec((rows, gw), lambda b, i: (b * nsb + i, width // gw + 1)),
            pl.BlockSpec((rows, LANES), lambda b, i: (b * nsb + i, COL_SMALL // LANES)),
            pl.BlockSpec((rows, width), lambda b, i: (b * nsb + i, COL_Z // width)),
            pl.BlockSpec((1, LANES), const),
            pl.BlockSpec((1, LANES), const),
            pl.BlockSpec((1, width), const),
            pl.BlockSpec((1, width), const),
            pl.BlockSpec((LANES, width), const),
        ],
        out_specs=pl.BlockSpec((rows, width), row_map),
        out_shape=jax.ShapeDtypeStruct((t, width), BF16),
        scratch_shapes=[
            pltpu.VMEM((SSD_GROUPS, SSD_STATE, SSD_HEADS_PER_GROUP * SSD_HEADDIM), F32),
            pltpu.VMEM((rows, width), F32),
        ],
        compiler_params=_cparams(("parallel", "arbitrary")),
        name="ssd_scan",
    )(xbc, xbc, xbc, proj, proj, pad_small(dt_bias), pad_small(a_log), dexp,
      norm_g.reshape(1, width).astype(F32), e)


def _float_key(s):
    bits = pltpu.bitcast(s, I32)
    return bits ^ ((bits >> 31) & 0x7FFFFFFF)


ONES_ROWS = BF16_SUBLANES
VT_ROWS = ATT_HEADDIM + ONES_ROWS


def _dsa_kernel(q_ref, k_ref, v_ref, iq_ref, smq_ref, smk_ref, iko_ref, o_ref,
                key_scr, vt_scr, m_scr, acc_scr, qs_scr, s_scr, p_scr, *, tq, tk, seq, topk):
    qi = pl.program_id(1)
    q0 = qi * tq
    kend = q0 + tq
    nt = (kend + tk - 1) // tk
    idx_bits = max(1, (seq - 1).bit_length())
    heads = [slice(h * ATT_HEADDIM, (h + 1) * ATT_HEADDIM) for h in range(ATT_HEADS)]

    @pl.when(qi == 0)
    def _():
        for h, hs in enumerate(heads):
            for c0 in range(0, seq, tk):
                vt = v_ref[c0:c0 + tk, hs].astype(F32).T
                vt_scr[h, 0:ATT_HEADDIM, c0:c0 + tk] = vt.astype(BF16)
            vt_scr[h, ATT_HEADDIM:VT_ROWS, :] = jnp.ones((ONES_ROWS, seq), BF16)

    w_t = smq_ref[...].astype(F32).T
    lane = lax.broadcasted_iota(I32, (tk, LANES), 1)
    q_chunk = (q0 + lax.broadcasted_iota(I32, (tk, tq), 1)) // CHUNK
    k_local = lax.broadcasted_iota(I32, (tk, tq), 0)

    def score_tile(kt, carry):
        k0 = pl.multiple_of(kt * tk, tk)
        smk = smk_ref[pl.ds(k0, tk), :]
        ik_even = jnp.where(lane < IDX_DIM, smk, jnp.zeros_like(smk))
        ik_odd = iko_ref[pl.ds(k0, tk), :]
        acc = jnp.zeros((tk, tq), F32)
        for hp in range(IDX_HEADS // 2):
            iq2 = iq_ref[:, hp * LANES:(hp + 1) * LANES]
            le = jnp.maximum(_dot_nt(ik_even, iq2), 0.0)
            lo = jnp.maximum(_dot_nt(ik_odd, iq2), 0.0)
            r = SMALL_IW + 2 * hp
            acc = acc + w_t[r:r + 1, :] * le + w_t[r + 1:r + 2, :] * lo
        admissible = (k0 + k_local) // CHUNK <= q_chunk
        key_scr[pl.ds(k0, tk), :] = jnp.where(admissible, _float_key(acc), NEG_INF_KEY)
        return carry

    lax.fori_loop(0, nt, score_tile, 0)

    def count_keys(pred):
        def body(kt, cnt):
            k0 = pl.multiple_of(kt * tk, tk)
            hit = jnp.where(pred(key_scr[pl.ds(k0, tk), :], k0), 1, 0)
            return cnt + jnp.sum(hit.reshape(tk // 8, 8, tq), axis=0)
        cnt = lax.fori_loop(0, nt, body, jnp.zeros((8, tq), I32))
        return jnp.sum(cnt, axis=0, keepdims=True)

    def radix_step(it, t_u):
        c_u = t_u | lax.shift_left(jnp.int32(1), 31 - it)
        c_s = c_u ^ INT_MIN
        cnt = count_keys(lambda keys, k0: keys >= c_s)
        return jnp.where(cnt >= topk, c_u, t_u)

    t_u = lax.fori_loop(0, 32, radix_step, jnp.zeros((1, tq), I32))
    thr = t_u ^ INT_MIN

    n_gt = count_keys(lambda keys, k0: keys > thr)
    n_eq = count_keys(lambda keys, k0: keys == thr)
    need = topk - n_gt
    excess = (n_eq > need) & (thr > NEG_INF_KEY)

    def tie_limit():
        def idx_step(it, jb):
            c = jb | lax.shift_left(jnp.int32(1), idx_bits - 1 - it)
            cnt = count_keys(lambda keys, k0: (keys == thr) & ((k0 + k_local) < c))
            return jnp.where(cnt < need, c, jb)
        jb = lax.fori_loop(0, idx_bits, idx_step, jnp.zeros((1, tq), I32))
        return jnp.where(excess, jb, seq)

    j_lim = lax.cond(jnp.max(jnp.where(excess, 1, 0)) > 0, tie_limit, lambda: jnp.full((1, tq), seq, I32))

    def bias_tile(kt, carry):
        k0 = pl.multiple_of(kt * tk, tk)
        keys = key_scr[pl.ds(k0, tk), :]
        sel = (keys > thr) | ((keys == thr) & ((k0 + k_local) <= j_lim))
        sel = sel & (keys > NEG_INF_KEY)
        key_scr[pl.ds(k0, tk), :] = pltpu.bitcast(jnp.where(sel, 0.0, NEG_BIG).astype(F32), I32)
        return carry

    lax.fori_loop(0, nt, bias_tile, 0)

    m_scr[...] = jnp.full(m_scr.shape, NEG_BIG, F32)
    acc_scr[...] = jnp.zeros_like(acc_scr)
    qs_scr[...] = (q_ref[...].astype(F32) * (ATT_HEADDIM ** -0.5 * math.log2(math.e))).astype(BF16)

    def attn_tile(kt, carry):
        k0 = pl.multiple_of(kt * tk, tk)
        bias = pltpu.bitcast(key_scr[pl.ds(k0, tk), :], F32)
        tile_max = []
        for h, hs in enumerate(heads):
            s = _dot_nt(k_ref[pl.ds(k0, tk), hs], qs_scr[:, hs]) + bias
            s_scr[h] = s
            tile_max.append(jnp.max(s, axis=0, keepdims=True))
        m_new, alpha = [], []
        for h in range(ATT_HEADS):
            m_old = m_scr[h:h + 1, :]
            m_h = jnp.maximum(m_old, tile_max[h])
            m_scr[h:h + 1, :] = m_h
            m_new.append(m_h)
            alpha.append(jnp.exp2(m_old - m_h))
        for h in range(ATT_HEADS):
            p_scr[h] = jnp.exp2(s_scr[h] - m_new[h]).astype(BF16)
        for h in range(ATT_HEADS):
            pv = jnp.dot(vt_scr[h, :, pl.ds(k0, tk)], p_scr[h], preferred_element_type=F32)
            acc_scr[h] = alpha[h] * acc_scr[h] + pv
        return carry

    lax.fori_loop(0, nt, attn_tile, 0)
    for h, hs in enumerate(heads):
        acc = acc_scr[h]
        out_t = acc[0:ATT_HEADDIM, :] / acc[ATT_HEADDIM:ATT_HEADDIM + 1, :]
        o_ref[:, hs] = out_t.T.astype(o_ref.dtype)


def dsa_attention(proj, nb, seq, tq=256, tk=512):
    t = proj.shape[0]
    w = ATT_HEADS * ATT_HEADDIM
    tk = min(tk, seq)
    nqb = seq // tq
    topk = min(TOPK_MAX, seq // 4)
    qrow = lambda b, i: b * nqb + i
    once = pl.Buffered(1)
    return pl.pallas_call(
        functools.partial(_dsa_kernel, tq=tq, tk=tk, seq=seq, topk=topk),
        grid=(nb, nqb),
        in_specs=[
            pl.BlockSpec((tq, w), lambda b, i: (qrow(b, i), COL_Q // w)),
            pl.BlockSpec((seq, w), lambda b, i: (b, COL_K // w), pipeline_mode=once),
            pl.BlockSpec((seq, w), lambda b, i: (b, COL_V // w), pipeline_mode=once),
            pl.BlockSpec((tq, w), lambda b, i: (qrow(b, i), COL_IQ // w)),
            pl.BlockSpec((tq, LANES), lambda b, i: (qrow(b, i), COL_SMALL // LANES)),
            pl.BlockSpec((seq, LANES), lambda b, i: (b, COL_SMALL // LANES), pipeline_mode=once),
            pl.BlockSpec((seq, LANES), lambda b, i: (b, COL_IK_ODD // LANES), pipeline_mode=once),
        ],
        out_specs=pl.BlockSpec((tq, w), lambda b, i: (qrow(b, i), 0)),
        out_shape=jax.ShapeDtypeStruct((t, w), BF16),
        scratch_shapes=[
            pltpu.VMEM((seq, tq), I32),
            pltpu.VMEM((ATT_HEADS, VT_ROWS, seq), BF16),
            pltpu.VMEM((ATT_HEADS, tq), F32),
            pltpu.VMEM((ATT_HEADS, VT_ROWS, tq), F32),
            pltpu.VMEM((tq, w), BF16),
            pltpu.VMEM((ATT_HEADS, tk, tq), F32),
            pltpu.VMEM((ATT_HEADS, tk, tq), BF16),
        ],
        compiler_params=_cparams(("parallel", "arbitrary")),
        name="dsa",
    )(proj, proj, proj, proj, proj, proj, proj)


def _cast_weights_once(pairs):
    @pl.when(pl.program_id(1) == 0)
    def _():
        for w_ref, s_ref in pairs:
            s_ref[...] = w_ref[...].astype(BF16)


def _merge_kernel(ys_ref, yc_ref, ya_ref, g0_ref, g1_ref, g2_ref, bg_ref, w1_ref, w2_ref, w3_ref, o_ref,
                  s1, s2, s3):
    _cast_weights_once(((w1_ref, s1), (w2_ref, s2), (w3_ref, s3)))

    def gate(g_ref, r):
        return _sigmoid(g_ref[...].astype(F32) + bg_ref[r:r + 1, :])

    acc = gate(g0_ref, 0) * jnp.dot(ys_ref[...], s1[...], preferred_element_type=F32)
    acc = acc + gate(g1_ref, 1) * jnp.dot(yc_ref[...], s2[...], preferred_element_type=F32)
    acc = acc + gate(g2_ref, 2) * jnp.dot(ya_ref[...], s3[...], preferred_element_type=F32)
    o_ref[...] = acc.astype(o_ref.dtype)


def merge_branches(y_ssd, y_sc, y_att, proj, b_gate, w1, w2, w3, layer, tm=1024, tn=512):
    t, d = y_ssd.shape[0], w1.shape[2]
    gcol = COL_GATE // tn
    gstep = d // tn
    wspec = lambda w: pl.BlockSpec((None, w.shape[1], tn), lambda j, i: (layer, 0, j))
    return pl.pallas_call(
        _merge_kernel,
        grid=(d // tn, t // tm),
        in_specs=[
            pl.BlockSpec((tm, y_ssd.shape[1]), lambda j, i: (i, 0)),
            pl.BlockSpec((tm, y_sc.shape[1]), lambda j, i: (i, 0)),
            pl.BlockSpec((tm, y_att.shape[1]), lambda j, i: (i, 0)),
            pl.BlockSpec((tm, tn), lambda j, i: (i, gcol + j)),
            pl.BlockSpec((tm, tn), lambda j, i: (i, gcol + gstep + j)),
            pl.BlockSpec((tm, tn), lambda j, i: (i, gcol + 2 * gstep + j)),
            pl.BlockSpec((None, N_BRANCH, tn), lambda j, i: (layer, 0, j)),
            wspec(w1), wspec(w2), wspec(w3),
        ],
        out_specs=pl.BlockSpec((tm, tn), lambda j, i: (i, j)),
        out_shape=jax.ShapeDtypeStruct((t, d), BF16),
        scratch_shapes=[pltpu.VMEM((w.shape[1], tn), BF16) for w in (w1, w2, w3)],
        compiler_params=_cparams(("arbitrary", "arbitrary")),
        name="merge",
    )(y_ssd, y_sc, y_att, proj, proj, proj, b_gate, w1, w2, w3)


def _proj_residual_kernel(a_ref, w_ref, x_ref, mod_ref, o_ref, ws, *, gate_row):
    _cast_weights_once(((w_ref, ws),))
    y = jnp.dot(a_ref[...], ws[...], preferred_element_type=F32)
    o_ref[...] = x_ref[...] + mod_ref[gate_row:gate_row + 1, :] * y


def proj_residual(a, w_all, layer, x2, mod, seq, gate_row, tm, tn, name):
    t, k = a.shape
    d = w_all.shape[2]
    nsb = seq // tm
    return pl.pallas_call(
        functools.partial(_proj_residual_kernel, gate_row=gate_row),
        grid=(d // tn, t // tm),
        in_specs=[
            pl.BlockSpec((tm, k), lambda j, i: (i, 0)),
            pl.BlockSpec((None, k, tn), lambda j, i: (layer, 0, j)),
            pl.BlockSpec((tm, tn), lambda j, i: (i, j)),
            pl.BlockSpec((None, 6, tn), lambda j, i: (i // nsb, 0, j)),
        ],
        out_specs=pl.BlockSpec((tm, tn), lambda j, i: (i, j)),
        out_shape=jax.ShapeDtypeStruct((t, d), F32),
        scratch_shapes=[pltpu.VMEM((k, tn), BF16)],
        compiler_params=_cparams(("arbitrary", "arbitrary")),
        name=name,
    )(a, w_all, x2, mod)


def _ffn_up_kernel(h_ref, wg_ref, wu_ref, o_ref, sg, su):
    _cast_weights_once(((wg_ref, sg), (wu_ref, su)))
    h = h_ref[...]
    g = jnp.dot(h, sg[...], preferred_element_type=F32)
    u = jnp.dot(h, su[...], preferred_element_type=F32)
    o_ref[...] = (_silu(g) * u).astype(o_ref.dtype)


def ffn_up(h, wg_all, wu_all, layer, tm=1024, tn=512):
    t, k = h.shape
    n = wg_all.shape[2]
    wspec = pl.BlockSpec((None, k, tn), lambda j, i: (layer, 0, j))
    return pl.pallas_call(
        _ffn_up_kernel,
        grid=(n // tn, t // tm),
        in_specs=[pl.BlockSpec((tm, k), lambda j, i: (i, 0)), wspec, wspec],
        out_specs=pl.BlockSpec((tm, tn), lambda j, i: (i, j)),
        out_shape=jax.ShapeDtypeStruct((t, n), BF16),
        scratch_shapes=[pltpu.VMEM((k, tn), BF16), pltpu.VMEM((k, tn), BF16)],
        compiler_params=_cparams(("arbitrary", "arbitrary")),
        name="ffn_up",
    )(h, wg_all, wu_all)


W_IN_COLS = 19568
_W_IN_SEGMENTS = (
    (COL_Z, 0, 2048),
    (COL_GATE, 13424, 6144),
    (COL_XBC, 2048, 4096),
    (COL_SCB, 6176, 7168),
    (COL_SMALL, 13344, IDX_DIM),
    (COL_SMALL + SMALL_DT, 6144, SSD_HEADS),
    (COL_SMALL + SMALL_IW, 13408, IDX_HEADS),
    (COL_IK_ODD + IDX_DIM, 13344, IDX_DIM),
)
_W_IN_ZERO = ((COL_SMALL + SMALL_IW + IDX_HEADS, 16), (COL_IK_ODD, IDX_DIM))
_PREP_CHUNK = 1024


def _prep_w_in_kernel(w_ref, o_ref):
    for dst, src, width in _W_IN_SEGMENTS:
        for c in range(0, width, _PREP_CHUNK):
            n = min(_PREP_CHUNK, width - c)
            o_ref[:, dst + c:dst + c + n] = w_ref[:, src + c:src + c + n].astype(BF16)
    for dst, width in _W_IN_ZERO:
        o_ref[:, dst:dst + width] = jnp.zeros((o_ref.shape[0], width), BF16)


def prep_w_in(w_in, tk=128):
    nl, d, n = w_in.shape
    return pl.pallas_call(
        _prep_w_in_kernel,
        grid=(nl, d // tk),
        in_specs=[pl.BlockSpec((None, tk, n), lambda l, i: (l, i, 0))],
        out_specs=pl.BlockSpec((None, tk, N_PROJ), lambda l, i: (l, i, 0)),
        out_shape=jax.ShapeDtypeStruct((nl, d, N_PROJ), BF16),
        compiler_params=_cparams(("parallel", "parallel")),
        name="prep_w_in",
    )(w_in)


def kernel(x, c, w_ada, b_ada, g_mix, w_in, b_gate, ssd_conv_w, ssd_conv_b, ssd_dt_bias, ssd_a_log, ssd_d,
           ssd_norm_g, sc_conv_w, w_br_ssd, w_br_sc, w_br_att, w_out, g_ffn, w_ffn_gate, w_ffn_up, w_ffn_down,
           g_final):
    nb, seq, d = x.shape
    depth = w_in.shape[0]
    t = nb * seq
    assert d == D and w_in.shape[2] == W_IN_COLS

    mod_all = ada_mod(c, w_ada, b_ada).reshape(depth, nb, 6, d)
    w_in_p = prep_w_in(w_in)

    x2 = x.reshape(t, d)
    for l in range(depth):
        mod = mod_all[l]
        h = norm_mod(x2, g_mix[l], mod, seq, shift_row=0, scale_row=1)
        proj = matmul_bf16(h, w_in_p, l, tm=1024, tn=1408)
        xbc = ssd_conv(proj, ssd_conv_w[l], ssd_conv_b[l], nb, seq)
        y_ssd = ssd_scan(xbc, proj, ssd_dt_bias[l], ssd_a_log[l], ssd_d[l], ssd_norm_g[l], nb, seq)
        y_sc = short_conv(proj, sc_conv_w[l], nb, seq)
        y_att = dsa_attention(proj, nb, seq)
        merged = merge_branches(y_ssd, y_sc, y_att, proj, b_gate, w_br_ssd, w_br_sc, w_br_att, l)
        x2 = proj_residual(merged, w_out, l, x2, mod, seq, gate_row=2, tm=1024, tn=512, name="out_proj")
        h = norm_mod(x2, g_ffn[l], mod, seq, shift_row=3, scale_row=4)
        a = ffn_up(h, w_ffn_gate, w_ffn_up, l)
        x2 = proj_residual(a, w_ffn_down, l, x2, mod, seq, gate_row=5, tm=512, tn=512, name="ffn_down")
    return final_norm(x2, g_final).reshape(nb, seq, d)
```
